```python
import math
import jax
import jax.numpy as jnp
from jax import lax
import numpy as np

D_MODEL = 2048
BATCH = 1
SEQ = 16384
DEPTH = 1

MIX_WIDTH = D_MODEL
ML_HEADS = 4
ML_V_W = MIX_WIDTH // 2
ML_DV = ML_V_W // ML_HEADS
ML_DQK = ML_DV // 2
ML_QK_W = ML_HEADS * ML_DQK
ML_CHUNK = 64
CONV_K = 4
MB_W = MIX_WIDTH - ML_V_W
MB_HEADS = 8
MB_DH = MB_W // MB_HEADS
MB_BLOCK = 256
MB_TOPK = 3
MB_QCHUNK = 64
ROPE_THETA = 500000.0
ROPE_DIM = MB_DH // 4
D_FF = ((8 * D_MODEL + 3 * 256 - 1) // (3 * 256)) * 256
LN_EPS = 1e-5
DN_ALPHA = (2 * DEPTH) ** 0.25
DN_BETA = (8 * DEPTH) ** -0.25
IN_SIZES = (MB_W, MB_W, MB_W, ML_QK_W, ML_QK_W, ML_V_W, ML_V_W, ML_HEADS, ML_HEADS)
IN_OFFSETS = tuple(int(o) for o in np.cumsum(IN_SIZES)[:-1])
IN_WIDTH = int(sum(IN_SIZES))

kernel_name = "hymba_mlstm_moba_deepnorm_adaln"


def layer_norm(x, w, b):
    xf = x.astype(jnp.float32)
    mu = jnp.mean(xf, axis=-1, keepdims=True)
    var = jnp.mean(jnp.square(xf - mu), axis=-1, keepdims=True)
    return ((xf - mu) * lax.rsqrt(var + LN_EPS) * w + b).astype(x.dtype)


def rope_partial(x, pos):
    half = ROPE_DIM // 2
    inv = ROPE_THETA ** (-jnp.arange(half, dtype=jnp.float32) * 2.0 / ROPE_DIM)
    ang = pos.astype(jnp.float32)[:, None] * inv[None, :]
    cos = jnp.cos(ang)[None, :, None, :]
    sin = jnp.sin(ang)[None, :, None, :]
    xf = x[..., :ROPE_DIM].astype(jnp.float32)
    x1, x2 = xf[..., :half], xf[..., half:]
    rot = jnp.concatenate([x1 * cos - x2 * sin, x2 * cos + x1 * sin], axis=-1)
    return jnp.concatenate([rot.astype(x.dtype), x[..., ROPE_DIM:]], axis=-1)


def causal_conv(x, w, b):
    ch = x.shape[-1]
    y = lax.conv_general_dilated(x, w[:, None, :].astype(x.dtype), (1,), [(CONV_K - 1, 0)],
                                 dimension_numbers=('NWC', 'WIO', 'NWC'), feature_group_count=ch)
    return y + b.astype(x.dtype)


def moba_attention(q, k, v):
    B, S, H, Dh = q.shape
    nb = -(-S // MB_BLOCK)
    pad = nb * MB_BLOCK - S
    kp = jnp.pad(k, ((0, 0), (0, pad), (0, 0), (0, 0)))
    vp = jnp.pad(v, ((0, 0), (0, pad), (0, 0), (0, 0)))
    kblk = kp.reshape(B, nb, MB_BLOCK, H, Dh).transpose(0, 3, 1, 2, 4)
    vblk = vp.reshape(B, nb, MB_BLOCK, H, Dh).transpose(0, 3, 1, 2, 4)
    kmean = jnp.mean(kblk.astype(jnp.float32), axis=3)
    topk = min(MB_TOPK, nb)
    nq = S // MB_QCHUNK
    q_chunks = q.reshape(B, nq, MB_QCHUNK, H, Dh).transpose(1, 0, 2, 3, 4)
    scale = Dh ** -0.5
    b_ix = jnp.arange(B)[:, None, None, None]
    h_ix = jnp.arange(H)[None, None, :, None]

    def one_chunk(args):
        qc, ci = args
        q0 = ci * MB_QCHUNK
        blk = q0 // MB_BLOCK
        qpos = q0 + jnp.arange(MB_QCHUNK)
        gate = jnp.einsum('bqhd,bhnd->bqhn', qc.astype(jnp.float32), kmean)
        gate = jnp.where(jnp.arange(nb) < blk, gate, -jnp.inf)
        _, idx = lax.top_k(gate, topk)
        valid = jnp.arange(topk) < blk
        ksel = kblk[b_ix, h_ix, idx]
        vsel = vblk[b_ix, h_ix, idx]
        s_sel = jnp.einsum('bqhd,bqhjkd->bqhjk', qc, ksel).astype(jnp.float32) * scale
        s_sel = jnp.where(valid[:, None], s_sel, -jnp.inf)
        kown = lax.dynamic_index_in_dim(kblk, blk, axis=2, keepdims=False)
        vown = lax.dynamic_index_in_dim(vblk, blk, axis=2, keepdims=False)
        s_own = jnp.einsum('bqhd,bhkd->bqhk', qc, kown).astype(jnp.float32) * scale
        kpos = blk * MB_BLOCK + jnp.arange(MB_BLOCK)
        causal = (kpos[None, :] <= qpos[:, None])[None, :, None, :]
        s_own = jnp.where(causal, s_own, -jnp.inf)
        logits = jnp.concatenate([s_sel.reshape(B, MB_QCHUNK, H, topk * MB_BLOCK), s_own], axis=-1)
        p = jax.nn.softmax(logits, axis=-1)
        p_sel = p[..., :topk * MB_BLOCK].reshape(B, MB_QCHUNK, H, topk, MB_BLOCK).astype(v.dtype)
        p_own = p[..., topk * MB_BLOCK:].astype(v.dtype)
        return (jnp.einsum('bqhjk,bqhjkd->bqhd', p_sel, vsel)
                + jnp.einsum('bqhk,bhkd->bqhd', p_own, vown))

    out = lax.map(one_chunk, (q_chunks, jnp.arange(nq)))
    return out.transpose(1, 0, 2, 3, 4).reshape(B, S, H * Dh)


def _mlstm_chunk_step(carry, inp):
    c_st, n_st, m_st = carry
    q, k, v, ig, lf = inp
    L = q.shape[-2]
    causal = jnp.tril(jnp.ones((L, L), dtype=bool))
    b = jnp.cumsum(lf, axis=-1)
    a = b + m_st[..., None]
    d = jnp.where(causal, b[..., :, None] - b[..., None, :] + ig[..., None, :], -jnp.inf)
    m_t = jnp.maximum(a, jnp.max(d, axis=-1))
    s = jnp.einsum('bhtd,bhsd->bhts', q, k) * jnp.exp(d - m_t[..., None])
    w_inter = jnp.exp(a - m_t)
    num = w_inter[..., None] * jnp.einsum('bhtd,bhdv->bhtv', q, c_st) + jnp.einsum('bhts,bhsv->bhtv', s, v)
    den = w_inter * jnp.einsum('bhtd,bhd->bht', q, n_st) + jnp.sum(s, axis=-1)
    h = num / jnp.maximum(jnp.abs(den), jnp.exp(-m_t))[..., None]
    b_last = b[..., -1]
    g = b_last[..., None] - b + ig
    m_new = jnp.maximum(b_last + m_st, jnp.max(g, axis=-1))
    w_k = jnp.exp(g - m_new[..., None])
    decay = jnp.exp(b_last + m_st - m_new)
    c_new = decay[..., None, None] * c_st + jnp.einsum('bhs,bhsd,bhsv->bhdv', w_k, k, v)
    n_new = decay[..., None] * n_st + jnp.einsum('bhs,bhsd->bhd', w_k, k)
    return (c_new, n_new, m_new), h


def mlstm(q, k, v, ig, fg):
    B, S, H, _ = q.shape
    nc = S // ML_CHUNK

    def to_chunks(t):
        t = t.astype(jnp.float32).reshape((B, nc, ML_CHUNK, H) + t.shape[3:])
        return jnp.moveaxis(t, (1, 3), (0, 2))

    lf = jax.nn.log_sigmoid(fg.astype(jnp.float32))
    init = (jnp.zeros((B, H, ML_DQK, ML_DV), jnp.float32),
            jnp.zeros((B, H, ML_DQK), jnp.float32),
            jnp.zeros((B, H), jnp.float32))
    _, h = lax.scan(_mlstm_chunk_step, init,
                    (to_chunks(q), to_chunks(k), to_chunks(v), to_chunks(ig), to_chunks(lf)))
    h = jnp.moveaxis(h, (0, 2), (1, 3))
    return h.reshape(B, S, H, ML_DV)


def setup_inputs(seed: int = 0) -> dict:
    key = jax.random.key(seed)
    ks = jax.random.split(key, 20)
    f32 = jnp.float32
    nrm = lambda k, shape, s: jax.random.normal(k, shape, f32) * s
    col_scale = np.ones((IN_WIDTH,), np.float32)
    col_scale[2 * MB_W:3 * MB_W] = DN_BETA
    v0 = 3 * MB_W + 2 * ML_QK_W
    col_scale[v0:v0 + ML_V_W] = DN_BETA
    return {
        "x": nrm(ks[0], (BATCH, SEQ, D_MODEL), 1.0),
        "c": nrm(ks[1], (BATCH, D_MODEL), 1.0),
        "w_ada": nrm(ks[2], (DEPTH, D_MODEL, 6 * D_MODEL), 0.5 * D_MODEL ** -0.5),
        "b_ada": nrm(ks[3], (DEPTH, 6 * D_MODEL), 0.02),
        "w_in": nrm(ks[4], (DEPTH, D_MODEL, IN_WIDTH), D_MODEL ** -0.5) * jnp.asarray(col_scale),
        "conv_w": nrm(ks[5], (DEPTH, CONV_K, 2 * ML_QK_W), CONV_K ** -0.5),
        "conv_b": nrm(ks[6], (DEPTH, 2 * ML_QK_W), 0.02),
        "ml_igate_b": nrm(ks[7], (DEPTH, ML_HEADS), 0.1),
        "ml_fgate_b": jnp.linspace(3.0, 6.0, ML_HEADS, dtype=f32)[None, :] + nrm(ks[8], (DEPTH, ML_HEADS), 0.01),
        "ml_norm_w": 1.0 + nrm(ks[9], (DEPTH, ML_V_W), 0.02),
        "w_out": nrm(ks[10], (DEPTH, MIX_WIDTH, D_MODEL), DN_BETA * MIX_WIDTH ** -0.5),
        "ln1_w": 1.0 + nrm(ks[11], (DEPTH, D_MODEL), 0.02),
        "ln1_b": nrm(ks[12], (DEPTH, D_MODEL), 0.02),
        "w_gu": nrm(ks[13], (DEPTH, D_MODEL, 2 * D_FF), DN_BETA * D_MODEL ** -0.5),
        "w_down": nrm(ks[14], (DEPTH, D_FF, D_MODEL), DN_BETA * D_FF ** -0.5),
        "ln2_w": 1.0 + nrm(ks[15], (DEPTH, D_MODEL), 0.02),
        "ln2_b": nrm(ks[16], (DEPTH, D_MODEL), 0.02),
    }


def reference(x, c, w_ada, b_ada, w_in, conv_w, conv_b, ml_igate_b, ml_fgate_b, ml_norm_w,
              w_out, ln1_w, ln1_b, w_gu, w_down, ln2_w, ln2_b):
    B, S, D = x.shape
    pos = jnp.arange(S)
    for l in range(DEPTH):
        ada = jax.nn.silu(c) @ w_ada[l] + b_ada[l]
        shift1, scale1, gate1, shift2, scale2, gate2 = [a[:, None, :] for a in jnp.split(ada, 6, axis=-1)]

        u = x * (1.0 + scale1) + shift1
        proj = u @ w_in[l]
        aq, ak, av, mq, mk, mv, mo, mi, mf = jnp.split(proj, IN_OFFSETS, axis=-1)

        aq = rope_partial(aq.reshape(B, S, MB_HEADS, MB_DH), pos)
        ak = rope_partial(ak.reshape(B, S, MB_HEADS, MB_DH), pos)
        a_out = moba_attention(aq, ak, av.reshape(B, S, MB_HEADS, MB_DH))

        qk = jax.nn.silu(causal_conv(jnp.concatenate([mq, mk], axis=-1), conv_w[l], conv_b[l]))
        mq, mk = jnp.split(qk, 2, axis=-1)
        h = mlstm(mq.reshape(B, S, ML_HEADS, ML_DQK),
                  mk.reshape(B, S, ML_HEADS, ML_DQK) * (ML_DQK ** -0.5),
                  mv.reshape(B, S, ML_HEADS, ML_DV),
                  mi + ml_igate_b[l], mf + ml_fgate_b[l])
        mu = jnp.mean(h, axis=-1, keepdims=True)
        var = jnp.mean(jnp.square(h - mu), axis=-1, keepdims=True)
        h = ((h - mu) * lax.rsqrt(var + LN_EPS)).reshape(B, S, ML_V_W) * ml_norm_w[l]
        m_out = (h * jax.nn.sigmoid(mo.astype(jnp.float32))).astype(x.dtype)

        mix = jnp.concatenate([m_out, a_out.astype(x.dtype)], axis=-1) @ w_out[l]
        x = layer_norm(DN_ALPHA * x + (1.0 + gate1) * mix, ln1_w[l], ln1_b[l])

        u = x * (1.0 + scale2) + shift2
        g, up = jnp.split(u @ w_gu[l], 2, axis=-1)
        f = (jax.nn.silu(g) * up) @ w_down[l]
        x = layer_norm(DN_ALPHA * x + (1.0 + gate2) * f, ln2_w[l], ln2_b[l])
    return x
```

```python
import functools

import numpy as np
import jax
import jax.numpy as jnp
from jax import lax
from jax.experimental import pallas as pl
from jax.experimental.pallas import tpu as pltpu

F32 = jnp.float32
BF16 = jnp.bfloat16

ML_HEADS = 4
ML_DQK = 128
ML_DV = 256
ML_QK_W = ML_HEADS * ML_DQK
ML_V_W = ML_HEADS * ML_DV
CONV_K = 4
MB_HEADS = 8
MB_DH = 128
MB_W = MB_HEADS * MB_DH
MB_BLOCK = 256
MB_TOPK = 3
ROPE_THETA = 500000.0
ROPE_DIM = MB_DH // 4
LN_EPS = 1e-5

V7X_LANES = 128
V7X_SUBLANES = 8
V7X_VMEM_BYTES = 64 * 1024 * 1024
VMEM_LIMIT = V7X_VMEM_BYTES - 8 * 1024 * 1024

MASK_BIG = 1e30
M_INIT = -3e38
LOG2E = 1.4426950408889634

GROUP_W = 1024
N_GROUPS = 6


def _params(sem):
    return pltpu.CompilerParams(dimension_semantics=sem, vmem_limit_bytes=VMEM_LIMIT)


def _layer_norm(y, w, b):
    mu = jnp.mean(y, axis=-1, keepdims=True)
    yc = y - mu
    var = jnp.mean(yc * yc, axis=-1, keepdims=True)
    return yc * lax.rsqrt(var + LN_EPS) * w + b


def _ada_kernel(c_ref, w_ref, b_ref, o_ref):
    s = jax.nn.silu(c_ref[...])
    o_ref[...] = jnp.sum(w_ref[...] * s, axis=0, keepdims=True) + b_ref[...]


def _ada(c, w, b, tn=1024):
    d, n = w.shape
    return pl.pallas_call(
        _ada_kernel,
        grid=(n // tn,),
        in_specs=[pl.BlockSpec((d, 1), lambda j: (0, 0)),
                  pl.BlockSpec((d, tn), lambda j: (0, j)),
                  pl.BlockSpec((1, tn), lambda j: (0, j))],
        out_specs=pl.BlockSpec((1, tn), lambda j: (0, j)),
        out_shape=jax.ShapeDtypeStruct((1, n), F32),
        compiler_params=_params(("arbitrary",)),
        name="ada",
    )(c.reshape(d, 1), w, b.reshape(1, n))


def _rope_heads(acc, cos, sina, sinb):
    half = ROPE_DIM // 2
    outs = []
    for h in range(MB_HEADS):
        xh = acc[:, h * MB_DH:(h + 1) * MB_DH]
        outs.append(xh * cos + pltpu.roll(xh, MB_DH - half, 1) * sina + pltpu.roll(xh, half, 1) * sinb)
    return outs


def _inproj_kernel(x_ref, mod_ref, w_ref, wg_ref, cos_ref, sina_ref, sinb_ref,
                   q_ref, kaug_ref, v_ref, mqk_ref, mv_ref, mo_ref, g_ref, kmean_ref,
                   u_ref, *, tm, qscale):
    i = pl.program_id(0)
    j = pl.program_id(1)

    @pl.when(j == 0)
    def _():
        u = x_ref[...] * (1.0 + mod_ref[1:2, :]) + mod_ref[0:1, :]
        ub = u.astype(BF16)
        u_ref[...] = ub
        g_ref[...] = jnp.dot(ub, wg_ref[...], preferred_element_type=F32)

    acc = jnp.dot(u_ref[...], w_ref[...], preferred_element_type=F32)

    @pl.when(j == 0)
    def _():
        rot = _rope_heads(acc, cos_ref[...], sina_ref[...], sinb_ref[...])
        for h in range(MB_HEADS):
            q_ref[:, h * MB_DH:(h + 1) * MB_DH] = (rot[h] * qscale).astype(BF16)

    @pl.when(j == 1)
    def _():
        rot = _rope_heads(acc, cos_ref[...], sina_ref[...], sinb_ref[...])
        row = lax.broadcasted_iota(jnp.int32, (tm, V7X_LANES), 0)
        lane = lax.broadcasted_iota(jnp.int32, (tm, V7X_LANES), 1)
        blk = (i * tm + row) // MB_BLOCK
        onehot = jnp.where(lane == blk, 1.0, 0.0).astype(BF16)
        for h in range(MB_HEADS):
            kaug_ref[:, 2 * h * MB_DH:(2 * h + 1) * MB_DH] = rot[h].astype(BF16)
            kaug_ref[:, (2 * h + 1) * MB_DH:(2 * h + 2) * MB_DH] = onehot
            for b in range(tm // MB_BLOCK):
                kmean_ref[0, b:b + 1, h * MB_DH:(h + 1) * MB_DH] = jnp.mean(
                    rot[h][b * MB_BLOCK:(b + 1) * MB_BLOCK, :], axis=0, keepdims=True)

    @pl.when(j == 2)
    def _():
        v_ref[...] = acc.astype(BF16)

    @pl.when(j == 3)
    def _():
        mqk_ref[...] = acc

    @pl.when(j == 4)
    def _():
        mv_ref[...] = acc.astype(BF16)

    @pl.when(j == 5)
    def _():
        mo_ref[...] = acc


def _inproj(x, mod, w_main, w_gates, cos, sina, sinb, tm):
    s, d = x.shape
    nbt = tm // MB_BLOCK
    row = lambda i, j: (i, 0)
    out_shape = (
        jax.ShapeDtypeStruct((s, MB_W), BF16),
        jax.ShapeDtypeStruct((s, 2 * MB_W), BF16),
        jax.ShapeDtypeStruct((s, MB_W), BF16),
        jax.ShapeDtypeStruct((s, 2 * ML_QK_W), F32),
        jax.ShapeDtypeStruct((s, ML_V_W), BF16),
        jax.ShapeDtypeStruct((s, ML_V_W), F32),
        jax.ShapeDtypeStruct((s, V7X_LANES), F32),
        jax.ShapeDtypeStruct((s // tm, nbt, MB_W), F32),
    )
    out_specs = (
        pl.BlockSpec((tm, MB_W), row),
        pl.BlockSpec((tm, 2 * MB_W), row),
        pl.BlockSpec((tm, MB_W), row),
        pl.BlockSpec((tm, 2 * ML_QK_W), row),
        pl.BlockSpec((tm, ML_V_W), row),
        pl.BlockSpec((tm, ML_V_W), row),
        pl.BlockSpec((tm, V7X_LANES), row),
        pl.BlockSpec((1, nbt, MB_W), lambda i, j: (i, 0, 0)),
    )
    in_specs = [
        pl.BlockSpec((tm, d), row),
        pl.BlockSpec((8, d), lambda i, j: (0, 0)),
        pl.BlockSpec((d, GROUP_W), lambda i, j: (0, j)),
        pl.BlockSpec((d, V7X_LANES), lambda i, j: (0, 0)),
        pl.BlockSpec((tm, MB_DH), row),
        pl.BlockSpec((tm, MB_DH), row),
        pl.BlockSpec((tm, MB_DH), row),
    ]
    qscale = MB_DH ** -0.5 * LOG2E
    return pl.pallas_call(
        functools.partial(_inproj_kernel, tm=tm, qscale=qscale),
        grid=(s // tm, N_GROUPS),
        in_specs=in_specs,
        out_specs=out_specs,
        out_shape=out_shape,
        scratch_shapes=[pltpu.VMEM((tm, d), BF16)],
        compiler_params=_params(("arbitrary", "arbitrary")),
        name="inproj",
    )(x, mod, w_main, w_gates, cos, sina, sinb)


def _mlstm_kernel(mqk_ref, mv_ref, g_ref, mo_ref, cw_ref, cb_ref, gb_ref, nw_ref, o_ref,
                  xext_ref, c_ref, n_ref, m_ref, *, lc):
    step = pl.program_id(0)

    @pl.when(step == 0)
    def _():
        xext_ref[0:8, :] = jnp.zeros((8, 2 * ML_QK_W), F32)
        c_ref[...] = jnp.zeros(c_ref.shape, F32)
        n_ref[...] = jnp.zeros(n_ref.shape, F32)
        m_ref[...] = jnp.zeros(m_ref.shape, F32)

    xext_ref[8:8 + lc, :] = mqk_ref[...]
    conv = cb_ref[...] + cw_ref[CONV_K - 1:CONV_K, :] * mqk_ref[...]
    for j in range(CONV_K - 1):
        off = 8 - (CONV_K - 1) + j
        conv = conv + cw_ref[j:j + 1, :] * xext_ref[off:off + lc, :]
    xext_ref[0:8, :] = xext_ref[lc:lc + 8, :]
    qk = jax.nn.silu(conv)

    gates = g_ref[...] + gb_ref[...]
    lf = jax.nn.log_sigmoid(gates)
    row = lax.broadcasted_iota(jnp.int32, (lc, lc), 0)
    col = lax.broadcasted_iota(jnp.int32, (lc, lc), 1)
    causal = col <= row
    tri = jnp.where(causal, 1.0, 0.0).astype(F32)
    bcum = jnp.dot(tri, lf, precision=lax.Precision.HIGHEST, preferred_element_type=F32)
    lane = lax.broadcasted_iota(jnp.int32, (lc, V7X_LANES), 1)
    packed = jnp.where(lane < ML_HEADS, gates, bcum)
    packed_t = packed.T

    for h in range(ML_HEADS):
        q = qk[:, h * ML_DQK:(h + 1) * ML_DQK]
        k = qk[:, ML_QK_W + h * ML_DQK:ML_QK_W + (h + 1) * ML_DQK] * (ML_DQK ** -0.5)
        v = mv_ref[:, h * ML_DV:(h + 1) * ML_DV]
        qb = q.astype(BF16)
        b_col = bcum[:, ML_HEADS + h:ML_HEADS + h + 1]
        ig_col = gates[:, h:h + 1]
        b_row = packed_t[ML_HEADS + h:ML_HEADS + h + 1, :]
        ig_row = packed_t[h:h + 1, :]
        m_prev = m_ref[h:h + 1, 0:1]
        c_prev = c_ref[h]
        n_prev = n_ref[h:h + 1, :]

        dmat = jnp.where(causal, b_col - (b_row - ig_row), -jnp.inf)
        a = b_col + m_prev
        m_t = jnp.maximum(a, jnp.max(dmat, axis=1, keepdims=True))
        e = jnp.exp(dmat - m_t)
        qkt = lax.dot_general(qb, k.astype(BF16), (((1,), (1,)), ((), ())), preferred_element_type=F32)
        smat = qkt * e
        w_inter = jnp.exp(a - m_t)
        num = (w_inter * jnp.dot(qb, c_prev.astype(BF16), preferred_element_type=F32)
               + jnp.dot(smat.astype(BF16), v, preferred_element_type=F32))
        den = w_inter * jnp.sum(q * n_prev, axis=1, keepdims=True) + jnp.sum(smat, axis=1, keepdims=True)
        hh = num / jnp.maximum(jnp.abs(den), jnp.exp(-m_t))

        mu = jnp.mean(hh, axis=1, keepdims=True)
        hc = hh - mu
        var = jnp.mean(hc * hc, axis=1, keepdims=True)
        sl = slice(h * ML_DV, (h + 1) * ML_DV)
        hn = hc * lax.rsqrt(var + LN_EPS) * nw_ref[:, sl]
        o_ref[:, sl] = (hn * jax.nn.sigmoid(mo_ref[:, sl])).astype(BF16)

        b_last = b_col[lc - 1:lc, :]
        g_col = b_last - b_col + ig_col
        m_new = jnp.maximum(b_last + m_prev, jnp.max(g_col, axis=0, keepdims=True))
        w_k = jnp.exp(g_col - m_new)
        decay = jnp.exp(b_last + m_prev - m_new)
        kw = k * w_k
        upd = lax.dot_general(kw.astype(BF16), v, (((0,), (0,)), ((), ())), preferred_element_type=F32)
        c_ref[h] = decay * c_prev + upd
        n_ref[h:h + 1, :] = decay * n_prev + jnp.sum(kw, axis=0, keepdims=True)
        m_ref[h:h + 1, :] = jnp.broadcast_to(m_new, (1, V7X_LANES))


def _mlstm(mqk, mv, gates, mo, conv_w, conv_b, gate_b, norm_w, lc):
    s = mqk.shape[0]
    row = lambda i: (i, 0)
    fixed = lambda i: (0, 0)
    return pl.pallas_call(
        functools.partial(_mlstm_kernel, lc=lc),
        grid=(s // lc,),
        in_specs=[pl.BlockSpec((lc, 2 * ML_QK_W), row),
                  pl.BlockSpec((lc, ML_V_W), row),
                  pl.BlockSpec((lc, V7X_LANES), row),
                  pl.BlockSpec((lc, ML_V_W), row),
                  pl.BlockSpec((CONV_K, 2 * ML_QK_W), fixed),
                  pl.BlockSpec((1, 2 * ML_QK_W), fixed),
                  pl.BlockSpec((1, V7X_LANES), fixed),
                  pl.BlockSpec((1, ML_V_W), fixed)],
        out_specs=pl.BlockSpec((lc, ML_V_W), row),
        out_shape=jax.ShapeDtypeStruct((s, ML_V_W), BF16),
        scratch_shapes=[pltpu.VMEM((lc + 8, 2 * ML_QK_W), F32),
                        pltpu.VMEM((ML_HEADS, ML_DQK, ML_DV), F32),
                        pltpu.VMEM((8, ML_DQK), F32),
                        pltpu.VMEM((8, V7X_LANES), F32)],
        compiler_params=_params(("arbitrary",)),
        name="mlstm",
    )(mqk, mv, gates, mo, conv_w, conv_b, gate_b, norm_w)


def _moba_kernel(q_ref, kaug_ref, v_ref, kmt_ref, o_ref, *, tq):
    i = pl.program_id(1)
    q = q_ref[...]
    km = kmt_ref[0]
    km_hi = km.astype(BF16)
    km_lo = (km - km_hi.astype(F32)).astype(BF16)
    gate = (jnp.dot(q, km_hi, preferred_element_type=F32)
            + jnp.dot(q, km_lo, preferred_element_type=F32))
    blk = lax.broadcasted_iota(jnp.int32, gate.shape, 1)
    blk_f = blk.astype(F32)
    gm = jnp.where(blk < i, gate, -jnp.inf)
    keep = blk == i
    for r in range(MB_TOPK):
        mx = jnp.max(gm, axis=1, keepdims=True)
        first = jnp.min(jnp.where(gm == mx, blk_f, float(V7X_LANES)), axis=1, keepdims=True)
        pick = blk_f == first
        keep = jnp.logical_or(keep, jnp.logical_and(pick, r < i))
        gm = jnp.where(pick, -jnp.inf, gm)
    qa = jnp.concatenate([q, jnp.where(keep, 0.0, -MASK_BIG).astype(BF16)], axis=1)

    def block_logits(j):
        start = pl.multiple_of(j * MB_BLOCK, MB_BLOCK)
        kb = kaug_ref[pl.ds(start, MB_BLOCK), :]
        vb = v_ref[pl.ds(start, MB_BLOCK), :]
        s = lax.dot_general(qa, kb, (((1,), (1,)), ((), ())), preferred_element_type=F32)
        return s, vb

    def update(carry, s, vb):
        m, l, acc = carry
        m_new = jnp.maximum(m, jnp.max(s, axis=1, keepdims=True))
        alpha = jnp.exp2(m - m_new)
        p = jnp.exp2(s - m_new)
        l = alpha * l + jnp.sum(p, axis=1, keepdims=True)
        acc = alpha * acc + jnp.dot(p.astype(BF16), vb, preferred_element_type=F32)
        return m_new, l, acc

    def past_block(j, carry):
        s, vb = block_logits(j)
        return update(carry, s, vb)

    init = (jnp.full((tq, 1), M_INIT, F32), jnp.zeros((tq, 1), F32), jnp.zeros((tq, MB_DH), F32))
    carry = lax.fori_loop(0, i, past_block, init)
    s, vb = block_logits(i)
    row = lax.broadcasted_iota(jnp.int32, s.shape, 0)
    col = lax.broadcasted_iota(jnp.int32, s.shape, 1)
    _, l, acc = update(carry, jnp.where(col <= row, s, -MASK_BIG), vb)
    o_ref[...] = (acc / l).astype(BF16)


def _moba(q, kaug, v, kmean_t):
    s = q.shape[0]
    tq = MB_BLOCK
    return pl.pallas_call(
        functools.partial(_moba_kernel, tq=tq),
        grid=(MB_HEADS, s // tq),
        in_specs=[pl.BlockSpec((tq, MB_DH), lambda h, i: (i, h)),
                  pl.BlockSpec((s, 2 * MB_DH), lambda h, i: (0, h)),
                  pl.BlockSpec((s, MB_DH), lambda h, i: (0, h)),
                  pl.BlockSpec((1, MB_DH, V7X_LANES), lambda h, i: (h, 0, 0))],
        out_specs=pl.BlockSpec((tq, MB_DH), lambda h, i: (i, h)),
        out_shape=jax.ShapeDtypeStruct((s, MB_W), BF16),
        compiler_params=_params(("arbitrary", "arbitrary")),
        name="moba",
    )(q, kaug, v, kmean_t)


def _outproj_kernel(m_ref, a_ref, x_ref, mod_ref, wm_ref, wa_ref, lnw_ref, lnb_ref, o_ref, *, alpha):
    mix = (jnp.dot(m_ref[...], wm_ref[...], preferred_element_type=F32)
           + jnp.dot(a_ref[...], wa_ref[...], preferred_element_type=F32))
    y = alpha * x_ref[...] + (1.0 + mod_ref[2:3, :]) * mix
    o_ref[...] = _layer_norm(y, lnw_ref[...], lnb_ref[...])


def _outproj(m_out, a_out, x, mod, w_m, w_a, ln_w, ln_b, alpha, tm):
    s, d = x.shape
    row = lambda i: (i, 0)
    fixed = lambda i: (0, 0)
    return pl.pallas_call(
        functools.partial(_outproj_kernel, alpha=alpha),
        grid=(s // tm,),
        in_specs=[pl.BlockSpec((tm, ML_V_W), row),
                  pl.BlockSpec((tm, MB_W), row),
                  pl.BlockSpec((tm, d), row),
                  pl.BlockSpec((8, d), fixed),
                  pl.BlockSpec((ML_V_W, d), fixed),
                  pl.BlockSpec((MB_W, d), fixed),
                  pl.BlockSpec((1, d), fixed),
                  pl.BlockSpec((1, d), fixed)],
        out_specs=pl.BlockSpec((tm, d), row),
        out_shape=jax.ShapeDtypeStruct((s, d), F32),
        compiler_params=_params(("arbitrary",)),
        name="outproj",
    )(m_out, a_out, x, mod, w_m, w_a, ln_w, ln_b)


def _ffn_kernel(x_ref, mod_ref, wg_ref, wu_ref, wd_ref, lnw_ref, lnb_ref, o_ref, u_ref, acc_ref, *, alpha):
    t = pl.program_id(1)

    @pl.when(t == 0)
    def _():
        u_ref[...] = (x_ref[...] * (1.0 + mod_ref[4:5, :]) + mod_ref[3:4, :]).astype(BF16)
        acc_ref[...] = jnp.zeros(acc_ref.shape, F32)

    u = u_ref[...]
    g = jnp.dot(u, wg_ref[...], preferred_element_type=F32)
    up = jnp.dot(u, wu_ref[...], preferred_element_type=F32)
    hid = (jax.nn.silu(g) * up).astype(BF16)
    acc_ref[...] += jnp.dot(hid, wd_ref[...], preferred_element_type=F32)

    @pl.when(t == pl.num_programs(1) - 1)
    def _():
        y = alpha * x_ref[...] + (1.0 + mod_ref[5:6, :]) * acc_ref[...]
        o_ref[...] = _layer_norm(y, lnw_ref[...], lnb_ref[...])


def _ffn(x1, mod, w_gu, w_down, ln_w, ln_b, alpha, tm, tf):
    s, d = x1.shape
    dff = w_down.shape[0]
    nt = dff // tf
    row = lambda i, t: (i, 0)
    fixed = lambda i, t: (0, 0)
    return pl.pallas_call(
        functools.partial(_ffn_kernel, alpha=alpha),
        grid=(s // tm, nt),
        in_specs=[pl.BlockSpec((tm, d), row),
                  pl.BlockSpec((8, d), fixed),
                  pl.BlockSpec((d, tf), lambda i, t: (0, t)),
                  pl.BlockSpec((d, tf), lambda i, t: (0, t + nt)),
                  pl.BlockSpec((tf, d), lambda i, t: (t, 0)),
                  pl.BlockSpec((1, d), fixed),
                  pl.BlockSpec((1, d), fixed)],
        out_specs=pl.BlockSpec((tm, d), row),
        out_shape=jax.ShapeDtypeStruct((s, d), F32),
        scratch_shapes=[pltpu.VMEM((tm, d), BF16), pltpu.VMEM((tm, d), F32)],
        compiler_params=_params(("arbitrary", "arbitrary")),
        name="ffn",
    )(x1, mod, w_gu, w_gu, w_down, ln_w, ln_b)


def _rope_tables(s):
    half = ROPE_DIM // 2
    inv = ROPE_THETA ** (-jnp.arange(half, dtype=F32) * 2.0 / ROPE_DIM)
    ang = jnp.arange(s).astype(F32)[:, None] * inv[None, :]
    cos, sin = jnp.cos(ang), jnp.sin(ang)
    zeros = jnp.zeros((s, half), F32)
    pad = MB_DH - ROPE_DIM
    cos_t = jnp.concatenate([cos, cos, jnp.ones((s, pad), F32)], axis=1)
    sina_t = jnp.concatenate([-sin, zeros, jnp.zeros((s, pad), F32)], axis=1)
    sinb_t = jnp.concatenate([zeros, sin, jnp.zeros((s, pad), F32)], axis=1)
    return cos_t, sina_t, sinb_t


def _pick_tile(n, target):
    t = min(n, target)
    assert n % t == 0, (n, t)
    return t


def kernel(x, c, w_ada, b_ada, w_in, conv_w, conv_b, ml_igate_b, ml_fgate_b, ml_norm_w,
           w_out, ln1_w, ln1_b, w_gu, w_down, ln2_w, ln2_b):
    bsz, s, d = x.shape
    depth = w_ada.shape[0]
    dff = w_down.shape[1]
    assert bsz == 1 and c.shape == (1, d)
    assert s % MB_BLOCK == 0 and s // MB_BLOCK <= V7X_LANES
    main_w = N_GROUPS * GROUP_W
    assert w_in.shape[2] == main_w + 2 * ML_HEADS
    alpha = float((2 * depth) ** 0.25)
    tm_in = _pick_tile(s, 512)
    tm_out = _pick_tile(s, 512)
    tm_ffn = _pick_tile(s, 512)
    tf = _pick_tile(dff, 512)
    lc = _pick_tile(s, 256)
    cos_t, sina_t, sinb_t = _rope_tables(s)
    nb = s // MB_BLOCK

    xs = x.reshape(s, d)
    for l in range(depth):
        ada = _ada(c, w_ada[l], b_ada[l])
        mod = jnp.pad(ada.reshape(6, d), ((0, 2), (0, 0)))
        w_main = w_in[l][:, :main_w].astype(BF16)
        w_gates = jnp.pad(w_in[l][:, main_w:], ((0, 0), (0, V7X_LANES - 2 * ML_HEADS))).astype(BF16)
        q, kaug, v, mqk, mv, mo, gates, kmean = _inproj(xs, mod, w_main, w_gates, cos_t, sina_t, sinb_t, tm_in)

        gate_b = jnp.pad(jnp.concatenate([ml_igate_b[l], ml_fgate_b[l]]), (0, V7X_LANES - 2 * ML_HEADS))
        m_out = _mlstm(mqk, mv, gates, mo, conv_w[l], conv_b[l].reshape(1, -1),
                       gate_b.reshape(1, V7X_LANES), ml_norm_w[l].reshape(1, -1), lc)

        kmean_t = kmean.reshape(nb, MB_HEADS, MB_DH).transpose(1, 2, 0)
        kmean_t = jnp.pad(kmean_t, ((0, 0), (0, 0), (0, V7X_LANES - nb)))
        a_out = _moba(q, kaug, v, kmean_t)

        wo = w_out[l].astype(BF16)
        x1 = _outproj(m_out, a_out, xs, mod, wo[:ML_V_W], wo[ML_V_W:], ln1_w[l].reshape(1, d),
                      ln1_b[l].reshape(1, d), alpha, tm_out)
        xs = _ffn(x1, mod, w_gu[l].astype(BF16), w_down[l].astype(BF16), ln2_w[l].reshape(1, d),
                  ln2_b[l].reshape(1, d), alpha, tm_ffn, tf)
    return xs.reshape(bsz, s, d)
```

```python
import functools

import numpy as np
import jax
import jax.numpy as jnp
from jax import lax
from jax.experimental import pallas as pl
from jax.experimental.pallas import tpu as pltpu

F32 = jnp.float32
BF16 = jnp.bfloat16

ML_HEADS = 4
ML_DQK = 128
ML_DV = 256
ML_QK_W = ML_HEADS * ML_DQK
ML_V_W = ML_HEADS * ML_DV
CONV_K = 4
MB_HEADS = 8
MB_DH = 128
MB_W = MB_HEADS * MB_DH
MB_BLOCK = 256
MB_TOPK = 3
ROPE_THETA = 500000.0
ROPE_DIM = MB_DH // 4
LN_EPS = 1e-5

V7X_LANES = 128
V7X_SUBLANES = 8
V7X_VMEM_BYTES = 64 * 1024 * 1024
VMEM_LIMIT = V7X_VMEM_BYTES - 8 * 1024 * 1024

MASK_BIG = 1e30
M_INIT = -3e38
LOG2E = 1.4426950408889634

GROUP_W = 1024
N_GROUPS = 6


def _params(sem):
    return pltpu.CompilerParams(dimension_semantics=sem, vmem_limit_bytes=VMEM_LIMIT)


def _layer_norm(y, w, b):
    mu = jnp.mean(y, axis=-1, keepdims=True)
    yc = y - mu
    var = jnp.mean(yc * yc, axis=-1, keepdims=True)
    return yc * lax.rsqrt(var + LN_EPS) * w + b


def _ada_kernel(c_ref, w_ref, b_ref, o_ref):
    s = jax.nn.silu(c_ref[...])
    o_ref[...] = jnp.sum(w_ref[...] * s, axis=0, keepdims=True) + b_ref[...]


def _ada(c, w, b, tn=1024):
    d, n = w.shape
    return pl.pallas_call(
        _ada_kernel,
        grid=(n // tn,),
        in_specs=[pl.BlockSpec((d, 1), lambda j: (0, 0)),
                  pl.BlockSpec((d, tn), lambda j: (0, j)),
                  pl.BlockSpec((1, tn), lambda j: (0, j))],
        out_specs=pl.BlockSpec((1, tn), lambda j: (0, j)),
        out_shape=jax.ShapeDtypeStruct((1, n), F32),
        compiler_params=_params(("arbitrary",)),
        name="ada",
    )(c.reshape(d, 1), w, b.reshape(1, n))


def _rope_heads(acc, cos, sina, sinb):
    half = ROPE_DIM // 2
    outs = []
    for h in range(MB_HEADS):
        xh = acc[:, h * MB_DH:(h + 1) * MB_DH]
        outs.append(xh * cos + pltpu.roll(xh, MB_DH - half, 1) * sina + pltpu.roll(xh, half, 1) * sinb)
    return outs


def _inproj_kernel(x_ref, mod_ref, w_ref, wg_ref, cos_ref, sina_ref, sinb_ref,
                   q_ref, kaug_ref, v_ref, mqk_ref, mv_ref, mo_ref, g_ref, kmean_ref,
                   u_ref, *, tm, qscale):
    i = pl.program_id(0)
    j = pl.program_id(1)

    @pl.when(j == 0)
    def _():
        u = x_ref[...] * (1.0 + mod_ref[1:2, :]) + mod_ref[0:1, :]
        ub = u.astype(BF16)
        u_ref[...] = ub
        g_ref[...] = jnp.dot(ub, wg_ref[...], preferred_element_type=F32)

    acc = jnp.dot(u_ref[...], w_ref[...], preferred_element_type=F32)

    @pl.when(j == 0)
    def _():
        rot = _rope_heads(acc, cos_ref[...], sina_ref[...], sinb_ref[...])
        for h in range(MB_HEADS):
            q_ref[:, h * MB_DH:(h + 1) * MB_DH] = (rot[h] * qscale).astype(BF16)

    @pl.when(j == 1)
    def _():
        rot = _rope_heads(acc, cos_ref[...], sina_ref[...], sinb_ref[...])
        row = lax.broadcasted_iota(jnp.int32, (tm, V7X_LANES), 0)
        lane = lax.broadcasted_iota(jnp.int32, (tm, V7X_LANES), 1)
        blk = (i * tm + row) // MB_BLOCK
        onehot = jnp.where(lane == blk, 1.0, 0.0).astype(BF16)
        for h in range(MB_HEADS):
            kaug_ref[:, 2 * h * MB_DH:(2 * h + 1) * MB_DH] = rot[h].astype(BF16)
            kaug_ref[:, (2 * h + 1) * MB_DH:(2 * h + 2) * MB_DH] = onehot
            for b in range(tm // MB_BLOCK):
                kmean_ref[0, b:b + 1, h * MB_DH:(h + 1) * MB_DH] = jnp.mean(
                    rot[h][b * MB_BLOCK:(b + 1) * MB_BLOCK, :], axis=0, keepdims=True)

    @pl.when(j == 2)
    def _():
        v_ref[...] = acc.astype(BF16)

    @pl.when(j == 3)
    def _():
        mqk_ref[...] = acc

    @pl.when(j == 4)
    def _():
        mv_ref[...] = acc.astype(BF16)

    @pl.when(j == 5)
    def _():
        mo_ref[...] = acc


def _inproj(x, mod, w_main, w_gates, cos, sina, sinb, tm):
    s, d = x.shape
    nbt = tm // MB_BLOCK
    row = lambda i, j: (i, 0)
    out_shape = (
        jax.ShapeDtypeStruct((s, MB_W), BF16),
        jax.ShapeDtypeStruct((s, 2 * MB_W), BF16),
        jax.ShapeDtypeStruct((s, MB_W), BF16),
        jax.ShapeDtypeStruct((s, 2 * ML_QK_W), F32),
        jax.ShapeDtypeStruct((s, ML_V_W), BF16),
        jax.ShapeDtypeStruct((s, ML_V_W), F32),
        jax.ShapeDtypeStruct((s, V7X_LANES), F32),
        jax.ShapeDtypeStruct((s // tm, nbt, MB_W), F32),
    )
    out_specs = (
        pl.BlockSpec((tm, MB_W), row),
        pl.BlockSpec((tm, 2 * MB_W), row),
        pl.BlockSpec((tm, MB_W), row),
        pl.BlockSpec((tm, 2 * ML_QK_W), row),
        pl.BlockSpec((tm, ML_V_W), row),
        pl.BlockSpec((tm, ML_V_W), row),
        pl.BlockSpec((tm, V7X_LANES), row),
        pl.BlockSpec((1, nbt, MB_W), lambda i, j: (i, 0, 0)),
    )
    in_specs = [
        pl.BlockSpec((tm, d), row),
        pl.BlockSpec((8, d), lambda i, j: (0, 0)),
        pl.BlockSpec((d, GROUP_W), lambda i, j: (0, j)),
        pl.BlockSpec((d, V7X_LANES), lambda i, j: (0, 0)),
        pl.BlockSpec((tm, MB_DH), row),
        pl.BlockSpec((tm, MB_DH), row),
        pl.BlockSpec((tm, MB_DH), row),
    ]
    qscale = MB_DH ** -0.5 * LOG2E
    return pl.pallas_call(
        functools.partial(_inproj_kernel, tm=tm, qscale=qscale),
        grid=(s // tm, N_GROUPS),
        in_specs=in_specs,
        out_specs=out_specs,
        out_shape=out_shape,
        scratch_shapes=[pltpu.VMEM((tm, d), BF16)],
        compiler_params=_params(("arbitrary", "arbitrary")),
        name="inproj",
    )(x, mod, w_main, w_gates, cos, sina, sinb)


def _mlstm_kernel(mqk_ref, mv_ref, g_ref, mo_ref, cw_ref, cb_ref, gb_ref, nw_ref, o_ref,
                  xext_ref, c_ref, n_ref, m_ref, *, lc):
    step = pl.program_id(0)

    @pl.when(step == 0)
    def _():
        xext_ref[0:8, :] = jnp.zeros((8, 2 * ML_QK_W), F32)
        c_ref[...] = jnp.zeros(c_ref.shape, F32)
        n_ref[...] = jnp.zeros(n_ref.shape, F32)
        m_ref[...] = jnp.zeros(m_ref.shape, F32)

    xext_ref[8:8 + lc, :] = mqk_ref[...]
    conv = cb_ref[...] + cw_ref[CONV_K - 1:CONV_K, :] * mqk_ref[...]
    for j in range(CONV_K - 1):
        off = 8 - (CONV_K - 1) + j
        conv = conv + cw_ref[j:j + 1, :] * xext_ref[off:off + lc, :]
    xext_ref[0:8, :] = xext_ref[lc:lc + 8, :]
    qk = jax.nn.silu(conv)

    gates = g_ref[...] + gb_ref[...]
    lf = jax.nn.log_sigmoid(gates)
    row = lax.broadcasted_iota(jnp.int32, (lc, lc), 0)
    col = lax.broadcasted_iota(jnp.int32, (lc, lc), 1)
    causal = col <= row
    tri = jnp.where(causal, 1.0, 0.0).astype(F32)
    bcum = jnp.dot(tri, lf, precision=lax.Precision.HIGHEST, preferred_element_type=F32)
    lane = lax.broadcasted_iota(jnp.int32, (lc, V7X_LANES), 1)
    packed = jnp.where(lane < ML_HEADS, gates, bcum)
    packed_t = packed.T

    for h in range(ML_HEADS):
        q = qk[:, h * ML_DQK:(h + 1) * ML_DQK]
        k = qk[:, ML_QK_W + h * ML_DQK:ML_QK_W + (h + 1) * ML_DQK] * (ML_DQK ** -0.5)
        v = mv_ref[:, h * ML_DV:(h + 1) * ML_DV]
        qb = q.astype(BF16)
        b_col = bcum[:, ML_HEADS + h:ML_HEADS + h + 1]
        ig_col = gates[:, h:h + 1]
        b_row = packed_t[ML_HEADS + h:ML_HEADS + h + 1, :]
        ig_row = packed_t[h:h + 1, :]
        m_prev = m_ref[h:h + 1, 0:1]
        c_prev = c_ref[h]
        n_prev = n_ref[h:h + 1, :]

        dmat = jnp.where(causal, b_col - (b_row - ig_row), -jnp.inf)
        a = b_col + m_prev
        m_t = jnp.maximum(a, jnp.max(dmat, axis=1, keepdims=True))
        e = jnp.exp(dmat - m_t)
        qkt = lax.dot_general(qb, k.astype(BF16), (((1,), (1,)), ((), ())), preferred_element_type=F32)
        smat = qkt * e
        w_inter = jnp.exp(a - m_t)
        num = (w_inter * jnp.dot(qb, c_prev.astype(BF16), preferred_element_type=F32)
               + jnp.dot(smat.astype(BF16), v, preferred_element_type=F32))
        den = w_inter * jnp.sum(q * n_prev, axis=1, keepdims=True) + jnp.sum(smat, axis=1, keepdims=True)
        hh = num / jnp.maximum(jnp.abs(den), jnp.exp(-m_t))

        mu = jnp.mean(hh, axis=1, keepdims=True)
        hc = hh - mu
        var = jnp.mean(hc * hc, axis=1, keepdims=True)
        sl = slice(h * ML_DV, (h + 1) * ML_DV)
        hn = hc * lax.rsqrt(var + LN_EPS) * nw_ref[:, sl]
        o_ref[:, sl] = (hn * jax.nn.sigmoid(mo_ref[:, sl])).astype(BF16)

        b_last = b_col[lc - 1:lc, :]
        g_col = b_last - b_col + ig_col
        m_new = jnp.maximum(b_last + m_prev, jnp.max(g_col, axis=0, keepdims=True))
        w_k = jnp.exp(g_col - m_new)
        decay = jnp.exp(b_last + m_prev - m_new)
        kw = k * w_k
        upd = lax.dot_general(kw.astype(BF16), v, (((0,), (0,)), ((), ())), preferred_element_type=F32)
        c_ref[h] = decay * c_prev + upd
        n_ref[h:h + 1, :] = decay * n_prev + jnp.sum(kw, axis=0, keepdims=True)
        m_ref[h:h + 1, :] = jnp.broadcast_to(m_new, (1, V7X_LANES))


def _mlstm(mqk, mv, gates, mo, conv_w, conv_b, gate_b, norm_w, lc):
    s = mqk.shape[0]
    row = lambda i: (i, 0)
    fixed = lambda i: (0, 0)
    return pl.pallas_call(
        functools.partial(_mlstm_kernel, lc=lc),
        grid=(s // lc,),
        in_specs=[pl.BlockSpec((lc, 2 * ML_QK_W), row),
                  pl.BlockSpec((lc, ML_V_W), row),
                  pl.BlockSpec((lc, V7X_LANES), row),
                  pl.BlockSpec((lc, ML_V_W), row),
                  pl.BlockSpec((CONV_K, 2 * ML_QK_W), fixed),
                  pl.BlockSpec((1, 2 * ML_QK_W), fixed),
                  pl.BlockSpec((1, V7X_LANES), fixed),
                  pl.BlockSpec((1, ML_V_W), fixed)],
        out_specs=pl.BlockSpec((lc, ML_V_W), row),
        out_shape=jax.ShapeDtypeStruct((s, ML_V_W), BF16),
        scratch_shapes=[pltpu.VMEM((lc + 8, 2 * ML_QK_W), F32),
                        pltpu.VMEM((ML_HEADS, ML_DQK, ML_DV), F32),
                        pltpu.VMEM((8, ML_DQK), F32),
                        pltpu.VMEM((8, V7X_LANES), F32)],
        compiler_params=_params(("arbitrary",)),
        name="mlstm",
    )(mqk, mv, gates, mo, conv_w, conv_b, gate_b, norm_w)


def _moba_kernel(q_ref, kaug_ref, v_ref, km_ref, o_ref, vt_ref, qa_ref, acc_ref, *, tq):
    i = pl.program_id(1)
    nsub = tq // MB_BLOCK
    n_steps = v_ref.shape[0] // tq

    @pl.when(i == 0)
    def _():
        for c in range(n_steps):
            vt_ref[c] = v_ref[c * tq:(c + 1) * tq, :].astype(F32).T.astype(BF16)

    qt = q_ref[...].astype(F32).T.astype(BF16)
    km = km_ref[0]
    km_hi = km.astype(BF16)
    km_lo = (km - km_hi.astype(F32)).astype(BF16)
    gate = (jnp.dot(km_hi, qt, preferred_element_type=F32)
            + jnp.dot(km_lo, qt, preferred_element_type=F32))
    blk = lax.broadcasted_iota(jnp.int32, gate.shape, 0)
    own = i * nsub + lax.broadcasted_iota(jnp.int32, gate.shape, 1) // MB_BLOCK
    blk_f = blk.astype(F32)
    gm = jnp.where(blk < own, gate, -jnp.inf)
    keep = blk == own
    for r in range(MB_TOPK):
        mx = jnp.max(gm, axis=0, keepdims=True)
        first = jnp.min(jnp.where(gm == mx, blk_f, float(V7X_LANES)), axis=0, keepdims=True)
        pick = blk_f == first
        keep = jnp.logical_or(keep, jnp.logical_and(pick, r < own))
        gm = jnp.where(pick, -jnp.inf, gm)
    qa_ref[:MB_DH, :] = qt
    qa_ref[MB_DH:, :] = jnp.where(keep, 0.0, -MASK_BIG).astype(BF16)
    acc_ref[...] = jnp.zeros(acc_ref.shape, F32)

    def step(j, carry, causal):
        m_old, l_old = carry
        start = pl.multiple_of(j * tq, tq)
        st = jnp.dot(kaug_ref[pl.ds(start, tq), :], qa_ref[...], preferred_element_type=F32)
        if causal:
            key = lax.broadcasted_iota(jnp.int32, st.shape, 0)
            qry = lax.broadcasted_iota(jnp.int32, st.shape, 1)
            st = jnp.where(key <= qry, st, -MASK_BIG)
        m_new = jnp.maximum(m_old, jnp.max(st, axis=0, keepdims=True))
        alpha = jnp.exp2(m_old - m_new)
        pt = jnp.exp2(st - m_new)
        l_new = alpha * l_old + jnp.sum(pt, axis=0, keepdims=True)
        acc_ref[...] = alpha * acc_ref[...] + jnp.dot(vt_ref[j], pt.astype(BF16), preferred_element_type=F32)
        return m_new, l_new

    init = (jnp.full((1, tq), M_INIT, F32), jnp.zeros((1, tq), F32))
    carry = lax.fori_loop(0, i, functools.partial(step, causal=False), init)
    _, l_fin = step(i, carry, True)
    o_ref[...] = (acc_ref[...] / l_fin).T.astype(BF16)


def _moba(q, kaug, v, kmean, tq):
    s = q.shape[0]
    return pl.pallas_call(
        functools.partial(_moba_kernel, tq=tq),
        grid=(MB_HEADS, s // tq),
        in_specs=[pl.BlockSpec((tq, MB_DH), lambda h, i: (i, h)),
                  pl.BlockSpec((s, 2 * MB_DH), lambda h, i: (0, h)),
                  pl.BlockSpec((s, MB_DH), lambda h, i: (0, h)),
                  pl.BlockSpec((1, V7X_LANES, MB_DH), lambda h, i: (h, 0, 0))],
        out_specs=pl.BlockSpec((tq, MB_DH), lambda h, i: (i, h)),
        out_shape=jax.ShapeDtypeStruct((s, MB_W), BF16),
        scratch_shapes=[pltpu.VMEM((s // tq, MB_DH, tq), BF16),
                        pltpu.VMEM((2 * MB_DH, tq), BF16),
                        pltpu.VMEM((MB_DH, tq), F32)],
        compiler_params=_params(("arbitrary", "arbitrary")),
        name="moba",
    )(q, kaug, v, kmean)


def _outproj_kernel(m_ref, a_ref, x_ref, mod_ref, wm_ref, wa_ref, lnw_ref, lnb_ref, o_ref, *, alpha):
    mix = (jnp.dot(m_ref[...], wm_ref[...], preferred_element_type=F32)
           + jnp.dot(a_ref[...], wa_ref[...], preferred_element_type=F32))
    y = alpha * x_ref[...] + (1.0 + mod_ref[2:3, :]) * mix
    o_ref[...] = _layer_norm(y, lnw_ref[...], lnb_ref[...])


def _outproj(m_out, a_out, x, mod, w_m, w_a, ln_w, ln_b, alpha, tm):
    s, d = x.shape
    row = lambda i: (i, 0)
    fixed = lambda i: (0, 0)
    return pl.pallas_call(
        functools.partial(_outproj_kernel, alpha=alpha),
        grid=(s // tm,),
        in_specs=[pl.BlockSpec((tm, ML_V_W), row),
                  pl.BlockSpec((tm, MB_W), row),
                  pl.BlockSpec((tm, d), row),
                  pl.BlockSpec((8, d), fixed),
                  pl.BlockSpec((ML_V_W, d), fixed),
                  pl.BlockSpec((MB_W, d), fixed),
                  pl.BlockSpec((1, d), fixed),
                  pl.BlockSpec((1, d), fixed)],
        out_specs=pl.BlockSpec((tm, d), row),
        out_shape=jax.ShapeDtypeStruct((s, d), F32),
        compiler_params=_params(("arbitrary",)),
        name="outproj",
    )(m_out, a_out, x, mod, w_m, w_a, ln_w, ln_b)


def _ffn_kernel(x_ref, mod_ref, wg_ref, wu_ref, wd_ref, lnw_ref, lnb_ref, o_ref, u_ref, acc_ref, *, alpha):
    t = pl.program_id(1)

    @pl.when(t == 0)
    def _():
        u_ref[...] = (x_ref[...] * (1.0 + mod_ref[4:5, :]) + mod_ref[3:4, :]).astype(BF16)
        acc_ref[...] = jnp.zeros(acc_ref.shape, F32)

    u = u_ref[...]
    g = jnp.dot(u, wg_ref[...], preferred_element_type=F32)
    up = jnp.dot(u, wu_ref[...], preferred_element_type=F32)
    hid = (jax.nn.silu(g) * up).astype(BF16)
    acc_ref[...] += jnp.dot(hid, wd_ref[...], preferred_element_type=F32)

    @pl.when(t == pl.num_programs(1) - 1)
    def _():
        y = alpha * x_ref[...] + (1.0 + mod_ref[5:6, :]) * acc_ref[...]
        o_ref[...] = _layer_norm(y, lnw_ref[...], lnb_ref[...])


def _ffn(x1, mod, w_gu, w_down, ln_w, ln_b, alpha, tm, tf):
    s, d = x1.shape
    dff = w_down.shape[0]
    nt = dff // tf
    row = lambda i, t: (i, 0)
    fixed = lambda i, t: (0, 0)
    return pl.pallas_call(
        functools.partial(_ffn_kernel, alpha=alpha),
        grid=(s // tm, nt),
        in_specs=[pl.BlockSpec((tm, d), row),
                  pl.BlockSpec((8, d), fixed),
                  pl.BlockSpec((d, tf), lambda i, t: (0, t)),
                  pl.BlockSpec((d, tf), lambda i, t: (0, t + nt)),
                  pl.BlockSpec((tf, d), lambda i, t: (t, 0)),
                  pl.BlockSpec((1, d), fixed),
                  pl.BlockSpec((1, d), fixed)],
        out_specs=pl.BlockSpec((tm, d), row),
        out_shape=jax.ShapeDtypeStruct((s, d), F32),
        scratch_shapes=[pltpu.VMEM((tm, d), BF16), pltpu.VMEM((tm, d), F32)],
        compiler_params=_params(("arbitrary", "arbitrary")),
        name="ffn",
    )(x1, mod, w_gu, w_gu, w_down, ln_w, ln_b)


def _rope_tables(s):
    half = ROPE_DIM // 2
    inv = ROPE_THETA ** (-jnp.arange(half, dtype=F32) * 2.0 / ROPE_DIM)
    ang = jnp.arange(s).astype(F32)[:, None] * inv[None, :]
    cos, sin = jnp.cos(ang), jnp.sin(ang)
    zeros = jnp.zeros((s, half), F32)
    pad = MB_DH - ROPE_DIM
    cos_t = jnp.concatenate([cos, cos, jnp.ones((s, pad), F32)], axis=1)
    sina_t = jnp.concatenate([-sin, zeros, jnp.zeros((s, pad), F32)], axis=1)
    sinb_t = jnp.concatenate([zeros, sin, jnp.zeros((s, pad), F32)], axis=1)
    return cos_t, sina_t, sinb_t


def _pick_tile(n, target):
    t = min(n, target)
    assert n % t == 0, (n, t)
    return t


def kernel(x, c, w_ada, b_ada, w_in, conv_w, conv_b, ml_igate_b, ml_fgate_b, ml_norm_w,
           w_out, ln1_w, ln1_b, w_gu, w_down, ln2_w, ln2_b):
    bsz, s, d = x.shape
    depth = w_ada.shape[0]
    dff = w_down.shape[1]
    assert bsz == 1 and c.shape == (1, d)
    assert s % MB_BLOCK == 0 and s // MB_BLOCK <= V7X_LANES
    main_w = N_GROUPS * GROUP_W
    assert w_in.shape[2] == main_w + 2 * ML_HEADS
    alpha = float((2 * depth) ** 0.25)
    tm_in = _pick_tile(s, 512)
    tm_out = _pick_tile(s, 512)
    tm_ffn = _pick_tile(s, 512)
    tf = _pick_tile(dff, 512)
    lc = _pick_tile(s, 256)
    tq_moba = _pick_tile(s, 512)
    cos_t, sina_t, sinb_t = _rope_tables(s)
    nb = s // MB_BLOCK

    xs = x.reshape(s, d)
    for l in range(depth):
        ada = _ada(c, w_ada[l], b_ada[l])
        mod = jnp.pad(ada.reshape(6, d), ((0, 2), (0, 0)))
        w_main = w_in[l][:, :main_w].astype(BF16)
        w_gates = jnp.pad(w_in[l][:, main_w:], ((0, 0), (0, V7X_LANES - 2 * ML_HEADS))).astype(BF16)
        q, kaug, v, mqk, mv, mo, gates, kmean = _inproj(xs, mod, w_main, w_gates, cos_t, sina_t, sinb_t, tm_in)

        gate_b = jnp.pad(jnp.concatenate([ml_igate_b[l], ml_fgate_b[l]]), (0, V7X_LANES - 2 * ML_HEADS))
        m_out = _mlstm(mqk, mv, gates, mo, conv_w[l], conv_b[l].reshape(1, -1),
                       gate_b.reshape(1, V7X_LANES), ml_norm_w[l].reshape(1, -1), lc)

        kmean_h = kmean.reshape(nb, MB_HEADS, MB_DH).transpose(1, 0, 2)
        kmean_h = jnp.pad(kmean_h, ((0, 0), (0, V7X_LANES - nb), (0, 0)))
        a_out = _moba(q, kaug, v, kmean_h, tq_moba)

        wo = w_out[l].astype(BF16)
        x1 = _outproj(m_out, a_out, xs, mod, wo[:ML_V_W], wo[ML_V_W:], ln1_w[l].reshape(1, d),
                      ln1_b[l].reshape(1, d), alpha, tm_out)
        xs = _ffn(x1, mod, w_gu[l].astype(BF16), w_down[l].astype(BF16), ln2_w[l].reshape(1, d),
                  ln2_b[l].reshape(1, d), alpha, tm_ffn, tf)
    return xs.reshape(bsz, s, d)
```

```python
import functools

import numpy as np
import jax
import jax.numpy as jnp
from jax import lax
from jax.experimental import pallas as pl
from jax.experimental.pallas import tpu as pltpu

F32 = jnp.float32
BF16 = jnp.bfloat16

ML_HEADS = 4
ML_DQK = 128
ML_DV = 256
ML_QK_W = ML_HEADS * ML_DQK
ML_V_W = ML_HEADS * ML_DV
CONV_K = 4
MB_HEADS = 8
MB_DH = 128
MB_W = MB_HEADS * MB_DH
MB_BLOCK = 256
MB_TOPK = 3
ROPE_THETA = 500000.0
ROPE_DIM = MB_DH // 4
LN_EPS = 1e-5

V7X_LANES = 128
V7X_SUBLANES = 8
V7X_VMEM_BYTES = 64 * 1024 * 1024
VMEM_LIMIT = V7X_VMEM_BYTES - 8 * 1024 * 1024

MASK_BIG = 1e30
M_INIT = -3e38
LOG2E = 1.4426950408889634

GROUP_W = 1024
N_GROUPS = 6


def _params(sem):
    return pltpu.CompilerParams(dimension_semantics=sem, vmem_limit_bytes=VMEM_LIMIT)


def _layer_norm(y, w, b):
    mu = jnp.mean(y, axis=-1, keepdims=True)
    yc = y - mu
    var = jnp.mean(yc * yc, axis=-1, keepdims=True)
    return yc * lax.rsqrt(var + LN_EPS) * w + b


def _ada_kernel(c_ref, w_ref, b_ref, o_ref):
    s = jax.nn.silu(c_ref[...])
    o_ref[...] = jnp.sum(w_ref[...] * s, axis=0, keepdims=True) + b_ref[...]


def _ada(c, w, b, tn=1024):
    d, n = w.shape
    return pl.pallas_call(
        _ada_kernel,
        grid=(n // tn,),
        in_specs=[pl.BlockSpec((d, 1), lambda j: (0, 0)),
                  pl.BlockSpec((d, tn), lambda j: (0, j)),
                  pl.BlockSpec((1, tn), lambda j: (0, j))],
        out_specs=pl.BlockSpec((1, tn), lambda j: (0, j)),
        out_shape=jax.ShapeDtypeStruct((1, n), F32),
        compiler_params=_params(("arbitrary",)),
        name="ada",
    )(c.reshape(d, 1), w, b.reshape(1, n))


def _rope_heads(acc, cos, sina, sinb):
    half = ROPE_DIM // 2
    outs = []
    for h in range(MB_HEADS):
        xh = acc[:, h * MB_DH:(h + 1) * MB_DH]
        outs.append(xh * cos + pltpu.roll(xh, MB_DH - half, 1) * sina + pltpu.roll(xh, half, 1) * sinb)
    return outs


def _inproj_kernel(x_ref, mod_ref, w_ref, wg_ref, cos_ref, sina_ref, sinb_ref,
                   q_ref, kaug_ref, v_ref, mqk_ref, mv_ref, mo_ref, g_ref, kmean_ref,
                   u_ref, *, tm, qscale):
    i = pl.program_id(0)
    j = pl.program_id(1)

    @pl.when(j == 0)
    def _():
        u = x_ref[...] * (1.0 + mod_ref[1:2, :]) + mod_ref[0:1, :]
        ub = u.astype(BF16)
        u_ref[...] = ub
        g_ref[...] = jnp.dot(ub, wg_ref[...], preferred_element_type=F32)

    acc = jnp.dot(u_ref[...], w_ref[...], preferred_element_type=F32)

    @pl.when(j == 0)
    def _():
        rot = _rope_heads(acc, cos_ref[...], sina_ref[...], sinb_ref[...])
        for h in range(MB_HEADS):
            q_ref[:, h * MB_DH:(h + 1) * MB_DH] = (rot[h] * qscale).astype(BF16)

    @pl.when(j == 1)
    def _():
        rot = _rope_heads(acc, cos_ref[...], sina_ref[...], sinb_ref[...])
        row = lax.broadcasted_iota(jnp.int32, (tm, V7X_LANES), 0)
        lane = lax.broadcasted_iota(jnp.int32, (tm, V7X_LANES), 1)
        blk = (i * tm + row) // MB_BLOCK
        onehot = jnp.where(lane == blk, 1.0, 0.0).astype(BF16)
        for h in range(MB_HEADS):
            kaug_ref[:, 2 * h * MB_DH:(2 * h + 1) * MB_DH] = rot[h].astype(BF16)
            kaug_ref[:, (2 * h + 1) * MB_DH:(2 * h + 2) * MB_DH] = onehot
            for b in range(tm // MB_BLOCK):
                kmean_ref[0, b:b + 1, h * MB_DH:(h + 1) * MB_DH] = jnp.mean(
                    rot[h][b * MB_BLOCK:(b + 1) * MB_BLOCK, :], axis=0, keepdims=True)

    @pl.when(j == 2)
    def _():
        v_ref[...] = acc.astype(BF16)

    @pl.when(j == 3)
    def _():
        mqk_ref[...] = acc

    @pl.when(j == 4)
    def _():
        mv_ref[...] = acc.astype(BF16)

    @pl.when(j == 5)
    def _():
        mo_ref[...] = acc


def _inproj(x, mod, w_main, w_gates, cos, sina, sinb, tm):
    s, d = x.shape
    nbt = tm // MB_BLOCK
    row = lambda i, j: (i, 0)
    out_shape = (
        jax.ShapeDtypeStruct((s, MB_W), BF16),
        jax.ShapeDtypeStruct((s, 2 * MB_W), BF16),
        jax.ShapeDtypeStruct((s, MB_W), BF16),
        jax.ShapeDtypeStruct((s, 2 * ML_QK_W), F32),
        jax.ShapeDtypeStruct((s, ML_V_W), BF16),
        jax.ShapeDtypeStruct((s, ML_V_W), F32),
        jax.ShapeDtypeStruct((s, V7X_LANES), F32),
        jax.ShapeDtypeStruct((s // tm, nbt, MB_W), F32),
    )
    out_specs = (
        pl.BlockSpec((tm, MB_W), row),
        pl.BlockSpec((tm, 2 * MB_W), row),
        pl.BlockSpec((tm, MB_W), row),
        pl.BlockSpec((tm, 2 * ML_QK_W), row),
        pl.BlockSpec((tm, ML_V_W), row),
        pl.BlockSpec((tm, ML_V_W), row),
        pl.BlockSpec((tm, V7X_LANES), row),
        pl.BlockSpec((1, nbt, MB_W), lambda i, j: (i, 0, 0)),
    )
    in_specs = [
        pl.BlockSpec((tm, d), row),
        pl.BlockSpec((8, d), lambda i, j: (0, 0)),
        pl.BlockSpec((d, GROUP_W), lambda i, j: (0, j)),
        pl.BlockSpec((d, V7X_LANES), lambda i, j: (0, 0)),
        pl.BlockSpec((tm, MB_DH), row),
        pl.BlockSpec((tm, MB_DH), row),
        pl.BlockSpec((tm, MB_DH), row),
    ]
    qscale = MB_DH ** -0.5 * LOG2E
    return pl.pallas_call(
        functools.partial(_inproj_kernel, tm=tm, qscale=qscale),
        grid=(s // tm, N_GROUPS),
        in_specs=in_specs,
        out_specs=out_specs,
        out_shape=out_shape,
        scratch_shapes=[pltpu.VMEM((tm, d), BF16)],
        compiler_params=_params(("arbitrary", "arbitrary")),
        name="inproj",
    )(x, mod, w_main, w_gates, cos, sina, sinb)


def _mlstm_kernel(mqk_ref, mv_ref, g_ref, mo_ref, cw_ref, cb_ref, gb_ref, nw_ref, o_ref,
                  xext_ref, c_ref, n_ref, m_ref, *, lc):
    step = pl.program_id(0)

    @pl.when(step == 0)
    def _():
        xext_ref[0:8, :] = jnp.zeros((8, 2 * ML_QK_W), F32)
        c_ref[...] = jnp.zeros(c_ref.shape, F32)
        n_ref[...] = jnp.zeros(n_ref.shape, F32)
        m_ref[...] = jnp.zeros(m_ref.shape, F32)

    xext_ref[8:8 + lc, :] = mqk_ref[...]
    conv = cb_ref[...] + cw_ref[CONV_K - 1:CONV_K, :] * mqk_ref[...]
    for j in range(CONV_K - 1):
        off = 8 - (CONV_K - 1) + j
        conv = conv + cw_ref[j:j + 1, :] * xext_ref[off:off + lc, :]
    xext_ref[0:8, :] = xext_ref[lc:lc + 8, :]
    qk = jax.nn.silu(conv)

    gates = g_ref[...] + gb_ref[...]
    lf = jax.nn.log_sigmoid(gates)
    row = lax.broadcasted_iota(jnp.int32, (lc, lc), 0)
    col = lax.broadcasted_iota(jnp.int32, (lc, lc), 1)
    causal = col <= row
    tri = jnp.where(causal, 1.0, 0.0).astype(F32)
    bcum = jnp.dot(tri, lf, precision=lax.Precision.HIGHEST, preferred_element_type=F32)
    lane = lax.broadcasted_iota(jnp.int32, (lc, V7X_LANES), 1)
    packed = jnp.where(lane < ML_HEADS, gates, bcum)
    packed_t = packed.T

    for h in range(ML_HEADS):
        q = qk[:, h * ML_DQK:(h + 1) * ML_DQK]
        k = qk[:, ML_QK_W + h * ML_DQK:ML_QK_W + (h + 1) * ML_DQK] * (ML_DQK ** -0.5)
        v = mv_ref[:, h * ML_DV:(h + 1) * ML_DV]
        qb = q.astype(BF16)
        b_col = bcum[:, ML_HEADS + h:ML_HEADS + h + 1]
        ig_col = gates[:, h:h + 1]
        b_row = packed_t[ML_HEADS + h:ML_HEADS + h + 1, :]
        ig_row = packed_t[h:h + 1, :]
        m_prev = m_ref[h:h + 1, 0:1]
        c_prev = c_ref[h]
        n_prev = n_ref[h:h + 1, :]

        dmat = jnp.where(causal, b_col - (b_row - ig_row), -jnp.inf)
        a = b_col + m_prev
        m_t = jnp.maximum(a, jnp.max(dmat, axis=1, keepdims=True))
        e = jnp.exp(dmat - m_t)
        qkt = lax.dot_general(qb, k.astype(BF16), (((1,), (1,)), ((), ())), preferred_element_type=F32)
        smat = qkt * e
        w_inter = jnp.exp(a - m_t)
        num = (w_inter * jnp.dot(qb, c_prev.astype(BF16), preferred_element_type=F32)
               + jnp.dot(smat.astype(BF16), v, preferred_element_type=F32))
        den = w_inter * jnp.sum(q * n_prev, axis=1, keepdims=True) + jnp.sum(smat, axis=1, keepdims=True)
        hh = num / jnp.maximum(jnp.abs(den), jnp.exp(-m_t))

        mu = jnp.mean(hh, axis=1, keepdims=True)
        hc = hh - mu
        var = jnp.mean(hc * hc, axis=1, keepdims=True)
        sl = slice(h * ML_DV, (h + 1) * ML_DV)
        hn = hc * lax.rsqrt(var + LN_EPS) * nw_ref[:, sl]
        o_ref[:, sl] = (hn * jax.nn.sigmoid(mo_ref[:, sl])).astype(BF16)

        b_last = b_col[lc - 1:lc, :]
        g_col = b_last - b_col + ig_col
        m_new = jnp.maximum(b_last + m_prev, jnp.max(g_col, axis=0, keepdims=True))
        w_k = jnp.exp(g_col - m_new)
        decay = jnp.exp(b_last + m_prev - m_new)
        kw = k * w_k
        upd = lax.dot_general(kw.astype(BF16), v, (((0,), (0,)), ((), ())), preferred_element_type=F32)
        c_ref[h] = decay * c_prev + upd
        n_ref[h:h + 1, :] = decay * n_prev + jnp.sum(kw, axis=0, keepdims=True)
        m_ref[h:h + 1, :] = jnp.broadcast_to(m_new, (1, V7X_LANES))


def _mlstm(mqk, mv, gates, mo, conv_w, conv_b, gate_b, norm_w, lc):
    s = mqk.shape[0]
    row = lambda i: (i, 0)
    fixed = lambda i: (0, 0)
    return pl.pallas_call(
        functools.partial(_mlstm_kernel, lc=lc),
        grid=(s // lc,),
        in_specs=[pl.BlockSpec((lc, 2 * ML_QK_W), row),
                  pl.BlockSpec((lc, ML_V_W), row),
                  pl.BlockSpec((lc, V7X_LANES), row),
                  pl.BlockSpec((lc, ML_V_W), row),
                  pl.BlockSpec((CONV_K, 2 * ML_QK_W), fixed),
                  pl.BlockSpec((1, 2 * ML_QK_W), fixed),
                  pl.BlockSpec((1, V7X_LANES), fixed),
                  pl.BlockSpec((1, ML_V_W), fixed)],
        out_specs=pl.BlockSpec((lc, ML_V_W), row),
        out_shape=jax.ShapeDtypeStruct((s, ML_V_W), BF16),
        scratch_shapes=[pltpu.VMEM((lc + 8, 2 * ML_QK_W), F32),
                        pltpu.VMEM((ML_HEADS, ML_DQK, ML_DV), F32),
                        pltpu.VMEM((8, ML_DQK), F32),
                        pltpu.VMEM((8, V7X_LANES), F32)],
        compiler_params=_params(("arbitrary",)),
        name="mlstm",
    )(mqk, mv, gates, mo, conv_w, conv_b, gate_b, norm_w)


def _moba_kernel(q_ref, kaug_ref, v_ref, km_ref, o_ref, vt_ref, qa_ref, acc_ref, *, tq, tk):
    i = pl.program_id(1)
    nsub = tq // MB_BLOCK

    @pl.when(i == 0)
    def _():
        for c in range(v_ref.shape[0] // tk):
            vt_ref[c] = v_ref[c * tk:(c + 1) * tk, :].astype(F32).T.astype(BF16)

    qt = q_ref[...].astype(F32).T.astype(BF16)
    km = km_ref[0]
    km_hi = km.astype(BF16)
    km_lo = (km - km_hi.astype(F32)).astype(BF16)
    gate = (jnp.dot(km_hi, qt, preferred_element_type=F32)
            + jnp.dot(km_lo, qt, preferred_element_type=F32))
    blk = lax.broadcasted_iota(jnp.int32, gate.shape, 0)
    own = i * nsub + lax.broadcasted_iota(jnp.int32, gate.shape, 1) // MB_BLOCK
    blk_f = blk.astype(F32)
    gm = jnp.where(blk < own, gate, -jnp.inf)
    keep = blk == own
    for r in range(MB_TOPK):
        mx = jnp.max(gm, axis=0, keepdims=True)
        first = jnp.min(jnp.where(gm == mx, blk_f, float(V7X_LANES)), axis=0, keepdims=True)
        pick = blk_f == first
        keep = jnp.logical_or(keep, jnp.logical_and(pick, r < own))
        gm = jnp.where(pick, -jnp.inf, gm)
    qa_ref[:MB_DH, :] = qt
    qa_ref[MB_DH:, :] = jnp.where(keep, 0.0, -MASK_BIG).astype(BF16)
    acc_ref[...] = jnp.zeros(acc_ref.shape, F32)

    def softmax_update(j, st, carry, c0):
        m_old, l_old = carry
        m_new = jnp.maximum(m_old, jnp.max(st, axis=0, keepdims=True))
        alpha = jnp.exp2(m_old - m_new)
        pt = jnp.exp2(st - m_new)
        l_new = alpha * l_old + jnp.sum(pt, axis=0, keepdims=True)
        acc_ref[:, c0:] = alpha * acc_ref[:, c0:] + jnp.dot(vt_ref[j], pt.astype(BF16),
                                                           preferred_element_type=F32)
        return m_new, l_new

    def logits(j, c0):
        start = pl.multiple_of(j * tk, tk)
        return jnp.dot(kaug_ref[pl.ds(start, tk), :], qa_ref[:, c0:], preferred_element_type=F32)

    def past_step(j, carry):
        return softmax_update(j, logits(j, 0), carry, 0)

    kpt = tq // tk
    m_run, l_run = lax.fori_loop(0, i * kpt, past_step,
                                 (jnp.full((1, tq), M_INIT, F32), jnp.zeros((1, tq), F32)))
    for d in range(kpt):
        c0 = d * tk
        st = logits(i * kpt + d, c0)
        key = lax.broadcasted_iota(jnp.int32, st.shape, 0)
        qry = lax.broadcasted_iota(jnp.int32, st.shape, 1)
        st = jnp.where(key <= qry, st, -MASK_BIG)
        m_sub, l_sub = softmax_update(i * kpt + d, st, (m_run[:, c0:], l_run[:, c0:]), c0)
        if c0:
            m_run = jnp.concatenate([m_run[:, :c0], m_sub], axis=1)
            l_run = jnp.concatenate([l_run[:, :c0], l_sub], axis=1)
        else:
            m_run, l_run = m_sub, l_sub
    o_ref[...] = (acc_ref[...] / l_run).T.astype(BF16)


def _moba(q, kaug, v, kmean, tq, tk):
    s = q.shape[0]
    assert tq % tk == 0 and tk % MB_BLOCK == 0
    return pl.pallas_call(
        functools.partial(_moba_kernel, tq=tq, tk=tk),
        grid=(MB_HEADS, s // tq),
        in_specs=[pl.BlockSpec((tq, MB_DH), lambda h, i: (i, h)),
                  pl.BlockSpec((s, 2 * MB_DH), lambda h, i: (0, h)),
                  pl.BlockSpec((s, MB_DH), lambda h, i: (0, h)),
                  pl.BlockSpec((1, V7X_LANES, MB_DH), lambda h, i: (h, 0, 0))],
        out_specs=pl.BlockSpec((tq, MB_DH), lambda h, i: (i, h)),
        out_shape=jax.ShapeDtypeStruct((s, MB_W), BF16),
        scratch_shapes=[pltpu.VMEM((s // tk, MB_DH, tk), BF16),
                        pltpu.VMEM((2 * MB_DH, tq), BF16),
                        pltpu.VMEM((MB_DH, tq), F32)],
        compiler_params=_params(("arbitrary", "arbitrary")),
        name="moba",
    )(q, kaug, v, kmean)


def _outproj_kernel(m_ref, a_ref, x_ref, mod_ref, wm_ref, wa_ref, lnw_ref, lnb_ref, o_ref, *, alpha):
    mix = (jnp.dot(m_ref[...], wm_ref[...], preferred_element_type=F32)
           + jnp.dot(a_ref[...], wa_ref[...], preferred_element_type=F32))
    y = alpha * x_ref[...] + (1.0 + mod_ref[2:3, :]) * mix
    o_ref[...] = _layer_norm(y, lnw_ref[...], lnb_ref[...])


def _outproj(m_out, a_out, x, mod, w_m, w_a, ln_w, ln_b, alpha, tm):
    s, d = x.shape
    row = lambda i: (i, 0)
    fixed = lambda i: (0, 0)
    return pl.pallas_call(
        functools.partial(_outproj_kernel, alpha=alpha),
        grid=(s // tm,),
        in_specs=[pl.BlockSpec((tm, ML_V_W), row),
                  pl.BlockSpec((tm, MB_W), row),
                  pl.BlockSpec((tm, d), row),
                  pl.BlockSpec((8, d), fixed),
                  pl.BlockSpec((ML_V_W, d), fixed),
                  pl.BlockSpec((MB_W, d), fixed),
                  pl.BlockSpec((1, d), fixed),
                  pl.BlockSpec((1, d), fixed)],
        out_specs=pl.BlockSpec((tm, d), row),
        out_shape=jax.ShapeDtypeStruct((s, d), F32),
        compiler_params=_params(("arbitrary",)),
        name="outproj",
    )(m_out, a_out, x, mod, w_m, w_a, ln_w, ln_b)


def _ffn_kernel(x_ref, mod_ref, wg_ref, wu_ref, wd_ref, lnw_ref, lnb_ref, o_ref, u_ref, acc_ref, *, alpha):
    t = pl.program_id(1)

    @pl.when(t == 0)
    def _():
        u_ref[...] = (x_ref[...] * (1.0 + mod_ref[4:5, :]) + mod_ref[3:4, :]).astype(BF16)
        acc_ref[...] = jnp.zeros(acc_ref.shape, F32)

    u = u_ref[...]
    g = jnp.dot(u, wg_ref[...], preferred_element_type=F32)
    up = jnp.dot(u, wu_ref[...], preferred_element_type=F32)
    hid = (jax.nn.silu(g) * up).astype(BF16)
    acc_ref[...] += jnp.dot(hid, wd_ref[...], preferred_element_type=F32)

    @pl.when(t == pl.num_programs(1) - 1)
    def _():
        y = alpha * x_ref[...] + (1.0 + mod_ref[5:6, :]) * acc_ref[...]
        o_ref[...] = _layer_norm(y, lnw_ref[...], lnb_ref[...])


def _ffn(x1, mod, w_gu, w_down, ln_w, ln_b, alpha, tm, tf):
    s, d = x1.shape
    dff = w_down.shape[0]
    nt = dff // tf
    row = lambda i, t: (i, 0)
    fixed = lambda i, t: (0, 0)
    return pl.pallas_call(
        functools.partial(_ffn_kernel, alpha=alpha),
        grid=(s // tm, nt),
        in_specs=[pl.BlockSpec((tm, d), row),
                  pl.BlockSpec((8, d), fixed),
                  pl.BlockSpec((d, tf), lambda i, t: (0, t)),
                  pl.BlockSpec((d, tf), lambda i, t: (0, t + nt)),
                  pl.BlockSpec((tf, d), lambda i, t: (t, 0)),
                  pl.BlockSpec((1, d), fixed),
                  pl.BlockSpec((1, d), fixed)],
        out_specs=pl.BlockSpec((tm, d), row),
        out_shape=jax.ShapeDtypeStruct((s, d), F32),
        scratch_shapes=[pltpu.VMEM((tm, d), BF16), pltpu.VMEM((tm, d), F32)],
        compiler_params=_params(("arbitrary", "arbitrary")),
        name="ffn",
    )(x1, mod, w_gu, w_gu, w_down, ln_w, ln_b)


def _rope_tables(s):
    half = ROPE_DIM // 2
    inv = ROPE_THETA ** (-jnp.arange(half, dtype=F32) * 2.0 / ROPE_DIM)
    ang = jnp.arange(s).astype(F32)[:, None] * inv[None, :]
    cos, sin = jnp.cos(ang), jnp.sin(ang)
    zeros = jnp.zeros((s, half), F32)
    pad = MB_DH - ROPE_DIM
    cos_t = jnp.concatenate([cos, cos, jnp.ones((s, pad), F32)], axis=1)
    sina_t = jnp.concatenate([-sin, zeros, jnp.zeros((s, pad), F32)], axis=1)
    sinb_t = jnp.concatenate([zeros, sin, jnp.zeros((s, pad), F32)], axis=1)
    return cos_t, sina_t, sinb_t


def _pick_tile(n, target):
    t = min(n, target)
    assert n % t == 0, (n, t)
    return t


def kernel(x, c, w_ada, b_ada, w_in, conv_w, conv_b, ml_igate_b, ml_fgate_b, ml_norm_w,
           w_out, ln1_w, ln1_b, w_gu, w_down, ln2_w, ln2_b):
    bsz, s, d = x.shape
    depth = w_ada.shape[0]
    dff = w_down.shape[1]
    assert bsz == 1 and c.shape == (1, d)
    assert s % MB_BLOCK == 0 and s // MB_BLOCK <= V7X_LANES
    main_w = N_GROUPS * GROUP_W
    assert w_in.shape[2] == main_w + 2 * ML_HEADS
    alpha = float((2 * depth) ** 0.25)
    tm_in = _pick_tile(s, 512)
    tm_out = _pick_tile(s, 512)
    tm_ffn = _pick_tile(s, 512)
    tf = _pick_tile(dff, 512)
    lc = _pick_tile(s, 256)
    tq_moba = _pick_tile(s, 2048)
    tk_moba = _pick_tile(tq_moba, 512)
    cos_t, sina_t, sinb_t = _rope_tables(s)
    nb = s // MB_BLOCK

    xs = x.reshape(s, d)
    for l in range(depth):
        ada = _ada(c, w_ada[l], b_ada[l])
        mod = jnp.pad(ada.reshape(6, d), ((0, 2), (0, 0)))
        w_main = w_in[l][:, :main_w].astype(BF16)
        w_gates = jnp.pad(w_in[l][:, main_w:], ((0, 0), (0, V7X_LANES - 2 * ML_HEADS))).astype(BF16)
        q, kaug, v, mqk, mv, mo, gates, kmean = _inproj(xs, mod, w_main, w_gates, cos_t, sina_t, sinb_t, tm_in)

        gate_b = jnp.pad(jnp.concatenate([ml_igate_b[l], ml_fgate_b[l]]), (0, V7X_LANES - 2 * ML_HEADS))
        m_out = _mlstm(mqk, mv, gates, mo, conv_w[l], conv_b[l].reshape(1, -1),
                       gate_b.reshape(1, V7X_LANES), ml_norm_w[l].reshape(1, -1), lc)

        kmean_h = kmean.reshape(nb, MB_HEADS, MB_DH).transpose(1, 0, 2)
        kmean_h = jnp.pad(kmean_h, ((0, 0), (0, V7X_LANES - nb), (0, 0)))
        a_out = _moba(q, kaug, v, kmean_h, tq_moba, tk_moba)

        wo = w_out[l].astype(BF16)
        x1 = _outproj(m_out, a_out, xs, mod, wo[:ML_V_W], wo[ML_V_W:], ln1_w[l].reshape(1, d),
                      ln1_b[l].reshape(1, d), alpha, tm_out)
        xs = _ffn(x1, mod, w_gu[l].astype(BF16), w_down[l].astype(BF16), ln2_w[l].reshape(1, d),
                  ln2_b[l].reshape(1, d), alpha, tm_ffn, tf)
    return xs.reshape(bsz, s, d)
```

```python
import functools

import numpy as np
import jax
import jax.numpy as jnp
from jax import lax
from jax.experimental import pallas as pl
from jax.experimental.pallas import tpu as pltpu

F32 = jnp.float32
BF16 = jnp.bfloat16

ML_HEADS = 4
ML_DQK = 128
ML_DV = 256
ML_QK_W = ML_HEADS * ML_DQK
ML_V_W = ML_HEADS * ML_DV
CONV_K = 4
MB_HEADS = 8
MB_DH = 128
MB_W = MB_HEADS * MB_DH
MB_BLOCK = 256
MB_TOPK = 3
ROPE_THETA = 500000.0
ROPE_DIM = MB_DH // 4
LN_EPS = 1e-5

V7X_LANES = 128
V7X_SUBLANES = 8
V7X_VMEM_BYTES = 64 * 1024 * 1024
VMEM_LIMIT = V7X_VMEM_BYTES - 8 * 1024 * 1024

MASK_BIG = 1e30
M_INIT = -3e38
LOG2E = 1.4426950408889634

GROUP_W = 1024
N_GROUPS = 6


def _params(sem):
    return pltpu.CompilerParams(dimension_semantics=sem, vmem_limit_bytes=VMEM_LIMIT)


def _layer_norm(y, w, b):
    mu = jnp.mean(y, axis=-1, keepdims=True)
    yc = y - mu
    var = jnp.mean(yc * yc, axis=-1, keepdims=True)
    return yc * lax.rsqrt(var + LN_EPS) * w + b


def _ada_kernel(c_ref, w_ref, b_ref, o_ref):
    s = jax.nn.silu(c_ref[...])
    o_ref[...] = jnp.sum(w_ref[...] * s, axis=0, keepdims=True) + b_ref[...]


def _ada(c, w, b, tn=1024):
    d, n = w.shape
    return pl.pallas_call(
        _ada_kernel,
        grid=(n // tn,),
        in_specs=[pl.BlockSpec((d, 1), lambda j: (0, 0)),
                  pl.BlockSpec((d, tn), lambda j: (0, j)),
                  pl.BlockSpec((1, tn), lambda j: (0, j))],
        out_specs=pl.BlockSpec((1, tn), lambda j: (0, j)),
        out_shape=jax.ShapeDtypeStruct((1, n), F32),
        compiler_params=_params(("arbitrary",)),
        name="ada",
    )(c.reshape(d, 1), w, b.reshape(1, n))


def _rope_heads(acc, cos, sina, sinb):
    half = ROPE_DIM // 2
    outs = []
    for h in range(MB_HEADS):
        xh = acc[:, h * MB_DH:(h + 1) * MB_DH]
        outs.append(xh * cos + pltpu.roll(xh, MB_DH - half, 1) * sina + pltpu.roll(xh, half, 1) * sinb)
    return outs


def _inproj_kernel(x_ref, mod_ref, w_ref, wg_ref, cos_ref, sina_ref, sinb_ref,
                   q_ref, kaug_ref, v_ref, mqk_ref, mv_ref, mo_ref, g_ref, kmean_ref,
                   u_ref, *, tm, qscale):
    i = pl.program_id(0)
    j = pl.program_id(1)

    @pl.when(j == 0)
    def _():
        u = x_ref[...] * (1.0 + mod_ref[1:2, :]) + mod_ref[0:1, :]
        ub = u.astype(BF16)
        u_ref[...] = ub
        g_ref[...] = jnp.dot(ub, wg_ref[...], preferred_element_type=F32)

    def proj():
        return jnp.dot(u_ref[...], w_ref[...], preferred_element_type=F32)

    @pl.when(j == 0)
    def _():
        rot = _rope_heads(proj(), cos_ref[...], sina_ref[...], sinb_ref[...])
        for h in range(MB_HEADS):
            q_ref[:, h * MB_DH:(h + 1) * MB_DH] = (rot[h] * qscale).astype(BF16)

    @pl.when(j == 1)
    def _():
        rot = _rope_heads(proj(), cos_ref[...], sina_ref[...], sinb_ref[...])
        row = lax.broadcasted_iota(jnp.int32, (tm, V7X_LANES), 0)
        lane = lax.broadcasted_iota(jnp.int32, (tm, V7X_LANES), 1)
        blk = (i * tm + row) // MB_BLOCK
        onehot = jnp.where(lane == blk, 1.0, 0.0).astype(BF16)
        for h in range(MB_HEADS):
            kaug_ref[:, 2 * h * MB_DH:(2 * h + 1) * MB_DH] = rot[h].astype(BF16)
            kaug_ref[:, (2 * h + 1) * MB_DH:(2 * h + 2) * MB_DH] = onehot
            for b in range(tm // MB_BLOCK):
                kmean_ref[0, b:b + 1, h * MB_DH:(h + 1) * MB_DH] = jnp.mean(
                    rot[h][b * MB_BLOCK:(b + 1) * MB_BLOCK, :], axis=0, keepdims=True)

    @pl.when(j == 2)
    def _():
        v_ref[...] = proj().astype(BF16)

    @pl.when(j == 3)
    def _():
        mqk_ref[...] = proj()

    @pl.when(j == 4)
    def _():
        mv_ref[...] = proj().astype(BF16)

    @pl.when(j == 5)
    def _():
        mo_ref[...] = proj()


def _inproj(x, mod, w_main, w_gates, cos, sina, sinb, tm):
    s, d = x.shape
    nbt = tm // MB_BLOCK
    row = lambda i, j: (i, 0)
    out_shape = (
        jax.ShapeDtypeStruct((s, MB_W), BF16),
        jax.ShapeDtypeStruct((s, 2 * MB_W), BF16),
        jax.ShapeDtypeStruct((s, MB_W), BF16),
        jax.ShapeDtypeStruct((s, 2 * ML_QK_W), F32),
        jax.ShapeDtypeStruct((s, ML_V_W), BF16),
        jax.ShapeDtypeStruct((s, ML_V_W), F32),
        jax.ShapeDtypeStruct((s, V7X_LANES), F32),
        jax.ShapeDtypeStruct((s // tm, nbt, MB_W), F32),
    )
    out_specs = (
        pl.BlockSpec((tm, MB_W), row),
        pl.BlockSpec((tm, 2 * MB_W), row),
        pl.BlockSpec((tm, MB_W), row),
        pl.BlockSpec((tm, 2 * ML_QK_W), row),
        pl.BlockSpec((tm, ML_V_W), row),
        pl.BlockSpec((tm, ML_V_W), row),
        pl.BlockSpec((tm, V7X_LANES), row),
        pl.BlockSpec((1, nbt, MB_W), lambda i, j: (i, 0, 0)),
    )
    in_specs = [
        pl.BlockSpec((tm, d), row),
        pl.BlockSpec((8, d), lambda i, j: (0, 0)),
        pl.BlockSpec((d, GROUP_W), lambda i, j: (0, j)),
        pl.BlockSpec((d, V7X_LANES), lambda i, j: (0, 0)),
        pl.BlockSpec((tm, MB_DH), row),
        pl.BlockSpec((tm, MB_DH), row),
        pl.BlockSpec((tm, MB_DH), row),
    ]
    qscale = MB_DH ** -0.5 * LOG2E
    return pl.pallas_call(
        functools.partial(_inproj_kernel, tm=tm, qscale=qscale),
        grid=(s // tm, N_GROUPS),
        in_specs=in_specs,
        out_specs=out_specs,
        out_shape=out_shape,
        scratch_shapes=[pltpu.VMEM((tm, d), BF16)],
        compiler_params=_params(("arbitrary", "arbitrary")),
        name="inproj",
    )(x, mod, w_main, w_gates, cos, sina, sinb)


def _mlstm_kernel(mqk_ref, mv_ref, g_ref, mo_ref, cw_ref, cb_ref, gb_ref, nw_ref, o_ref,
                  xext_ref, c_ref, n_ref, m_ref, *, lc):
    step = pl.program_id(0)

    @pl.when(step == 0)
    def _():
        xext_ref[0:8, :] = jnp.zeros((8, 2 * ML_QK_W), F32)
        c_ref[...] = jnp.zeros(c_ref.shape, F32)
        n_ref[...] = jnp.zeros(n_ref.shape, F32)
        m_ref[...] = jnp.zeros(m_ref.shape, F32)

    xext_ref[8:8 + lc, :] = mqk_ref[...]
    conv = cb_ref[...] + cw_ref[CONV_K - 1:CONV_K, :] * mqk_ref[...]
    for j in range(CONV_K - 1):
        off = 8 - (CONV_K - 1) + j
        conv = conv + cw_ref[j:j + 1, :] * xext_ref[off:off + lc, :]
    xext_ref[0:8, :] = xext_ref[lc:lc + 8, :]
    qk = jax.nn.silu(conv)

    gates = g_ref[...] + gb_ref[...]
    lf = jax.nn.log_sigmoid(gates)
    row = lax.broadcasted_iota(jnp.int32, (lc, lc), 0)
    col = lax.broadcasted_iota(jnp.int32, (lc, lc), 1)
    causal = col <= row
    tri = jnp.where(causal, 1.0, 0.0).astype(F32)
    bcum = jnp.dot(tri, lf, precision=lax.Precision.HIGHEST, preferred_element_type=F32)
    lane = lax.broadcasted_iota(jnp.int32, (lc, V7X_LANES), 1)
    packed = jnp.where(lane < ML_HEADS, gates, bcum)
    packed_t = packed.T

    for h in range(ML_HEADS):
        q = qk[:, h * ML_DQK:(h + 1) * ML_DQK]
        k = qk[:, ML_QK_W + h * ML_DQK:ML_QK_W + (h + 1) * ML_DQK] * (ML_DQK ** -0.5)
        v = mv_ref[:, h * ML_DV:(h + 1) * ML_DV]
        qb = q.astype(BF16)
        b_col = bcum[:, ML_HEADS + h:ML_HEADS + h + 1]
        ig_col = gates[:, h:h + 1]
        b_row = packed_t[ML_HEADS + h:ML_HEADS + h + 1, :]
        ig_row = packed_t[h:h + 1, :]
        m_prev = m_ref[h:h + 1, 0:1]
        c_prev = c_ref[h]
        n_prev = n_ref[h:h + 1, :]

        dmat = jnp.where(causal, b_col - (b_row - ig_row), -jnp.inf)
        a = b_col + m_prev
        m_t = jnp.maximum(a, jnp.max(dmat, axis=1, keepdims=True))
        e = jnp.exp(dmat - m_t)
        qkt = lax.dot_general(qb, k.astype(BF16), (((1,), (1,)), ((), ())), preferred_element_type=F32)
        smat = qkt * e
        w_inter = jnp.exp(a - m_t)
        num = (w_inter * jnp.dot(qb, c_prev.astype(BF16), preferred_element_type=F32)
               + jnp.dot(smat.astype(BF16), v, preferred_element_type=F32))
        den = w_inter * jnp.sum(q * n_prev, axis=1, keepdims=True) + jnp.sum(smat, axis=1, keepdims=True)
        hh = num / jnp.maximum(jnp.abs(den), jnp.exp(-m_t))

        mu = jnp.mean(hh, axis=1, keepdims=True)
        hc = hh - mu
        var = jnp.mean(hc * hc, axis=1, keepdims=True)
        sl = slice(h * ML_DV, (h + 1) * ML_DV)
        hn = hc * lax.rsqrt(var + LN_EPS) * nw_ref[:, sl]
        o_ref[:, sl] = (hn * jax.nn.sigmoid(mo_ref[:, sl])).astype(BF16)

        b_last = b_col[lc - 1:lc, :]
        g_col = b_last - b_col + ig_col
        m_new = jnp.maximum(b_last + m_prev, jnp.max(g_col, axis=0, keepdims=True))
        w_k = jnp.exp(g_col - m_new)
        decay = jnp.exp(b_last + m_prev - m_new)
        kw = k * w_k
        upd = lax.dot_general(kw.astype(BF16), v, (((0,), (0,)), ((), ())), preferred_element_type=F32)
        c_ref[h] = decay * c_prev + upd
        n_ref[h:h + 1, :] = decay * n_prev + jnp.sum(kw, axis=0, keepdims=True)
        m_ref[h:h + 1, :] = jnp.broadcast_to(m_new, (1, V7X_LANES))


def _mlstm(mqk, mv, gates, mo, conv_w, conv_b, gate_b, norm_w, lc):
    s = mqk.shape[0]
    row = lambda i: (i, 0)
    fixed = lambda i: (0, 0)
    return pl.pallas_call(
        functools.partial(_mlstm_kernel, lc=lc),
        grid=(s // lc,),
        in_specs=[pl.BlockSpec((lc, 2 * ML_QK_W), row),
                  pl.BlockSpec((lc, ML_V_W), row),
                  pl.BlockSpec((lc, V7X_LANES), row),
                  pl.BlockSpec((lc, ML_V_W), row),
                  pl.BlockSpec((CONV_K, 2 * ML_QK_W), fixed),
                  pl.BlockSpec((1, 2 * ML_QK_W), fixed),
                  pl.BlockSpec((1, V7X_LANES), fixed),
                  pl.BlockSpec((1, ML_V_W), fixed)],
        out_specs=pl.BlockSpec((lc, ML_V_W), row),
        out_shape=jax.ShapeDtypeStruct((s, ML_V_W), BF16),
        scratch_shapes=[pltpu.VMEM((lc + 8, 2 * ML_QK_W), F32),
                        pltpu.VMEM((ML_HEADS, ML_DQK, ML_DV), F32),
                        pltpu.VMEM((8, ML_DQK), F32),
                        pltpu.VMEM((8, V7X_LANES), F32)],
        compiler_params=_params(("arbitrary",)),
        name="mlstm",
    )(mqk, mv, gates, mo, conv_w, conv_b, gate_b, norm_w)


def _moba_kernel(q_ref, kaug_ref, v_ref, km_ref, o_ref, vt_ref, qa_ref, acc_ref, *, tq, tk):
    i = pl.program_id(1)
    nsub = tq // MB_BLOCK

    @pl.when(i == 0)
    def _():
        for c in range(v_ref.shape[0] // tk):
            vt_ref[c] = v_ref[c * tk:(c + 1) * tk, :].astype(F32).T.astype(BF16)

    qt = q_ref[...].astype(F32).T.astype(BF16)
    km = km_ref[0]
    km_hi = km.astype(BF16)
    km_lo = (km - km_hi.astype(F32)).astype(BF16)
    gate = (jnp.dot(km_hi, qt, preferred_element_type=F32)
            + jnp.dot(km_lo, qt, preferred_element_type=F32))
    blk = lax.broadcasted_iota(jnp.int32, gate.shape, 0)
    own = i * nsub + lax.broadcasted_iota(jnp.int32, gate.shape, 1) // MB_BLOCK
    blk_f = blk.astype(F32)
    gm = jnp.where(blk < own, gate, -jnp.inf)
    keep = blk == own
    for r in range(MB_TOPK):
        mx = jnp.max(gm, axis=0, keepdims=True)
        first = jnp.min(jnp.where(gm == mx, blk_f, float(V7X_LANES)), axis=0, keepdims=True)
        pick = blk_f == first
        keep = jnp.logical_or(keep, jnp.logical_and(pick, r < own))
        gm = jnp.where(pick, -jnp.inf, gm)
    qa_ref[:MB_DH, :] = qt
    qa_ref[MB_DH:, :] = jnp.where(keep, 0.0, -MASK_BIG).astype(BF16)
    acc_ref[...] = jnp.zeros(acc_ref.shape, F32)

    def softmax_update(j, st, carry, c0):
        m_old, l_old = carry
        m_new = jnp.maximum(m_old, jnp.max(st, axis=0, keepdims=True))
        alpha = jnp.exp2(m_old - m_new)
        pt = jnp.exp2(st - m_new)
        l_new = alpha * l_old + jnp.sum(pt, axis=0, keepdims=True)
        acc_ref[:, c0:] = alpha * acc_ref[:, c0:] + jnp.dot(vt_ref[j], pt.astype(BF16),
                                                           preferred_element_type=F32)
        return m_new, l_new

    def logits(j, c0):
        start = pl.multiple_of(j * tk, tk)
        return jnp.dot(kaug_ref[pl.ds(start, tk), :], qa_ref[:, c0:], preferred_element_type=F32)

    def past_step(j, carry):
        return softmax_update(j, logits(j, 0), carry, 0)

    kpt = tq // tk
    n_past = i * kpt
    m_run, l_run = lax.fori_loop(0, n_past, past_step,
                                 (jnp.full((1, tq), M_INIT, F32), jnp.zeros((1, tq), F32)))
    for d in range(kpt):
        c0 = d * tk
        st = logits(n_past + d, c0)
        key = lax.broadcasted_iota(jnp.int32, st.shape, 0)
        qry = lax.broadcasted_iota(jnp.int32, st.shape, 1)
        st = jnp.where(key <= qry, st, -MASK_BIG)
        m_sub, l_sub = softmax_update(n_past + d, st, (m_run[:, c0:], l_run[:, c0:]), c0)
        if c0:
            m_run = jnp.concatenate([m_run[:, :c0], m_sub], axis=1)
            l_run = jnp.concatenate([l_run[:, :c0], l_sub], axis=1)
        else:
            m_run, l_run = m_sub, l_sub
    o_ref[...] = (acc_ref[...] / l_run).T.astype(BF16)


def _moba(q, kaug, v, kmean, tq, tk):
    s = q.shape[0]
    assert tq % tk == 0 and tk % MB_BLOCK == 0
    return pl.pallas_call(
        functools.partial(_moba_kernel, tq=tq, tk=tk),
        grid=(MB_HEADS, s // tq),
        in_specs=[pl.BlockSpec((tq, MB_DH), lambda h, i: (i, h)),
                  pl.BlockSpec((s, 2 * MB_DH), lambda h, i: (0, h)),
                  pl.BlockSpec((s, MB_DH), lambda h, i: (0, h)),
                  pl.BlockSpec((1, V7X_LANES, MB_DH), lambda h, i: (h, 0, 0))],
        out_specs=pl.BlockSpec((tq, MB_DH), lambda h, i: (i, h)),
        out_shape=jax.ShapeDtypeStruct((s, MB_W), BF16),
        scratch_shapes=[pltpu.VMEM((s // tk, MB_DH, tk), BF16),
                        pltpu.VMEM((2 * MB_DH, tq), BF16),
                        pltpu.VMEM((MB_DH, tq), F32)],
        compiler_params=_params(("arbitrary", "arbitrary")),
        name="moba",
    )(q, kaug, v, kmean)


def _outproj_kernel(m_ref, a_ref, x_ref, mod_ref, wm_ref, wa_ref, lnw_ref, lnb_ref, o_ref, *, alpha):
    mix = (jnp.dot(m_ref[...], wm_ref[...], preferred_element_type=F32)
           + jnp.dot(a_ref[...], wa_ref[...], preferred_element_type=F32))
    y = alpha * x_ref[...] + (1.0 + mod_ref[2:3, :]) * mix
    o_ref[...] = _layer_norm(y, lnw_ref[...], lnb_ref[...])


def _outproj(m_out, a_out, x, mod, w_m, w_a, ln_w, ln_b, alpha, tm):
    s, d = x.shape
    row = lambda i: (i, 0)
    fixed = lambda i: (0, 0)
    return pl.pallas_call(
        functools.partial(_outproj_kernel, alpha=alpha),
        grid=(s // tm,),
        in_specs=[pl.BlockSpec((tm, ML_V_W), row),
                  pl.BlockSpec((tm, MB_W), row),
                  pl.BlockSpec((tm, d), row),
                  pl.BlockSpec((8, d), fixed),
                  pl.BlockSpec((ML_V_W, d), fixed),
                  pl.BlockSpec((MB_W, d), fixed),
                  pl.BlockSpec((1, d), fixed),
                  pl.BlockSpec((1, d), fixed)],
        out_specs=pl.BlockSpec((tm, d), row),
        out_shape=jax.ShapeDtypeStruct((s, d), F32),
        compiler_params=_params(("arbitrary",)),
        name="outproj",
    )(m_out, a_out, x, mod, w_m, w_a, ln_w, ln_b)


def _ffn_kernel(x_ref, mod_ref, wg_ref, wu_ref, wd_ref, lnw_ref, lnb_ref, o_ref, u_ref, acc_ref, *, alpha):
    t = pl.program_id(1)
    last = pl.num_programs(1) - 1

    def partial_down(u):
        g = jnp.dot(u, wg_ref[...], preferred_element_type=F32)
        up = jnp.dot(u, wu_ref[...], preferred_element_type=F32)
        hid = (jax.nn.silu(g) * up).astype(BF16)
        return jnp.dot(hid, wd_ref[...], preferred_element_type=F32)

    @pl.when(t == 0)
    def _():
        u = (x_ref[...] * (1.0 + mod_ref[4:5, :]) + mod_ref[3:4, :]).astype(BF16)
        u_ref[...] = u
        acc_ref[...] = partial_down(u)

    @pl.when(jnp.logical_and(t > 0, t < last))
    def _():
        acc_ref[...] += partial_down(u_ref[...])

    @pl.when(t == last)
    def _():
        f = acc_ref[...] + partial_down(u_ref[...])
        y = alpha * x_ref[...] + (1.0 + mod_ref[5:6, :]) * f
        o_ref[...] = _layer_norm(y, lnw_ref[...], lnb_ref[...])


def _ffn(x1, mod, w_gu, w_down, ln_w, ln_b, alpha, tm, tf):
    s, d = x1.shape
    dff = w_down.shape[0]
    nt = dff // tf
    assert nt >= 2
    row = lambda i, t: (i, 0)
    fixed = lambda i, t: (0, 0)
    return pl.pallas_call(
        functools.partial(_ffn_kernel, alpha=alpha),
        grid=(s // tm, nt),
        in_specs=[pl.BlockSpec((tm, d), row),
                  pl.BlockSpec((8, d), fixed),
                  pl.BlockSpec((d, tf), lambda i, t: (0, t)),
                  pl.BlockSpec((d, tf), lambda i, t: (0, t + nt)),
                  pl.BlockSpec((tf, d), lambda i, t: (t, 0)),
                  pl.BlockSpec((1, d), fixed),
                  pl.BlockSpec((1, d), fixed)],
        out_specs=pl.BlockSpec((tm, d), row),
        out_shape=jax.ShapeDtypeStruct((s, d), F32),
        scratch_shapes=[pltpu.VMEM((tm, d), BF16), pltpu.VMEM((tm, d), F32)],
        compiler_params=_params(("arbitrary", "arbitrary")),
        name="ffn",
    )(x1, mod, w_gu, w_gu, w_down, ln_w, ln_b)


def _rope_tables(s):
    half = ROPE_DIM // 2
    inv = ROPE_THETA ** (-np.arange(half, dtype=np.float64) * 2.0 / ROPE_DIM)
    ang = np.arange(s, dtype=np.float64)[:, None] * inv[None, :]
    cos, sin = jnp.asarray(np.cos(ang), F32), jnp.asarray(np.sin(ang), F32)
    zeros = jnp.zeros((s, half), F32)
    pad = MB_DH - ROPE_DIM
    cos_t = jnp.concatenate([cos, cos, jnp.ones((s, pad), F32)], axis=1)
    sina_t = jnp.concatenate([-sin, zeros, jnp.zeros((s, pad), F32)], axis=1)
    sinb_t = jnp.concatenate([zeros, sin, jnp.zeros((s, pad), F32)], axis=1)
    return cos_t, sina_t, sinb_t


def _pick_tile(n, target):
    t = min(n, target)
    assert n % t == 0, (n, t)
    return t


def kernel(x, c, w_ada, b_ada, w_in, conv_w, conv_b, ml_igate_b, ml_fgate_b, ml_norm_w,
           w_out, ln1_w, ln1_b, w_gu, w_down, ln2_w, ln2_b):
    bsz, s, d = x.shape
    depth = w_ada.shape[0]
    dff = w_down.shape[1]
    assert bsz == 1 and c.shape == (1, d)
    assert s % MB_BLOCK == 0 and s // MB_BLOCK <= V7X_LANES
    main_w = N_GROUPS * GROUP_W
    assert w_in.shape[2] == main_w + 2 * ML_HEADS
    alpha = float((2 * depth) ** 0.25)
    tm_in = _pick_tile(s, 512)
    tm_out = _pick_tile(s, 512)
    tm_ffn = _pick_tile(s, 512)
    tf = _pick_tile(dff, 512)
    lc = _pick_tile(s, 256)
    tq_moba = _pick_tile(s, 2048)
    tk_moba = _pick_tile(tq_moba, 512)
    cos_t, sina_t, sinb_t = _rope_tables(s)
    nb = s // MB_BLOCK

    xs = x.reshape(s, d)
    for l in range(depth):
        ada = _ada(c, w_ada[l], b_ada[l])
        mod = jnp.pad(ada.reshape(6, d), ((0, 2), (0, 0)))
        w_main = w_in[l].astype(BF16)
        w_gates = jnp.pad(w_in[l][:, main_w:], ((0, 0), (0, V7X_LANES - 2 * ML_HEADS))).astype(BF16)
        q, kaug, v, mqk, mv, mo, gates, kmean = _inproj(xs, mod, w_main, w_gates, cos_t, sina_t, sinb_t, tm_in)

        gate_b = jnp.pad(jnp.concatenate([ml_igate_b[l], ml_fgate_b[l]]), (0, V7X_LANES - 2 * ML_HEADS))
        m_out = _mlstm(mqk, mv, gates, mo, conv_w[l], conv_b[l].reshape(1, -1),
                       gate_b.reshape(1, V7X_LANES), ml_norm_w[l].reshape(1, -1), lc)

        kmean_h = kmean.reshape(nb, MB_HEADS, MB_DH).transpose(1, 0, 2)
        kmean_h = jnp.pad(kmean_h, ((0, 0), (0, V7X_LANES - nb), (0, 0)))
        a_out = _moba(q, kaug, v, kmean_h, tq_moba, tk_moba)

        wo = w_out[l].astype(BF16)
        x1 = _outproj(m_out, a_out, xs, mod, wo[:ML_V_W], wo[ML_V_W:], ln1_w[l].reshape(1, d),
                      ln1_b[l].reshape(1, d), alpha, tm_out)
        xs = _ffn(x1, mod, w_gu[l].astype(BF16), w_down[l].astype(BF16), ln2_w[l].reshape(1, d),
                  ln2_b[l].reshape(1, d), alpha, tm_ffn, tf)
    return xs.reshape(bsz, s, d)
```

```python
import functools

import numpy as np
import jax
import jax.numpy as jnp
from jax import lax
from jax.experimental import pallas as pl
from jax.experimental.pallas import tpu as pltpu

F32 = jnp.float32
BF16 = jnp.bfloat16

ML_HEADS = 4
ML_DQK = 128
ML_DV = 256
ML_QK_W = ML_HEADS * ML_DQK
ML_V_W = ML_HEADS * ML_DV
CONV_K = 4
MB_HEADS = 8
MB_DH = 128
MB_W = MB_HEADS * MB_DH
MB_BLOCK = 256
MB_TOPK = 3
ROPE_THETA = 500000.0
ROPE_DIM = MB_DH // 4
LN_EPS = 1e-5

V7X_LANES = 128
V7X_SUBLANES = 8
V7X_VMEM_BYTES = 64 * 1024 * 1024
VMEM_LIMIT = V7X_VMEM_BYTES - 8 * 1024 * 1024

MASK_BIG = 1e30
M_INIT = -3e38
LOG2E = 1.4426950408889634

GROUP_W = 1024
N_GROUPS = 6


def _params(sem):
    return pltpu.CompilerParams(dimension_semantics=sem, vmem_limit_bytes=VMEM_LIMIT)


def _layer_norm(y, w, b):
    mu = jnp.mean(y, axis=-1, keepdims=True)
    yc = y - mu
    var = jnp.mean(yc * yc, axis=-1, keepdims=True)
    return yc * lax.rsqrt(var + LN_EPS) * w + b


def _ada_kernel(c_ref, w_ref, b_ref, o_ref):
    s = jax.nn.silu(c_ref[...])
    o_ref[...] = jnp.sum(w_ref[...] * s, axis=0, keepdims=True) + b_ref[...]


def _ada(c, w, b, tn=1024):
    d, n = w.shape
    return pl.pallas_call(
        _ada_kernel,
        grid=(n // tn,),
        in_specs=[pl.BlockSpec((d, 1), lambda j: (0, 0)),
                  pl.BlockSpec((d, tn), lambda j: (0, j)),
                  pl.BlockSpec((1, tn), lambda j: (0, j))],
        out_specs=pl.BlockSpec((1, tn), lambda j: (0, j)),
        out_shape=jax.ShapeDtypeStruct((1, n), F32),
        compiler_params=_params(("arbitrary",)),
        name="ada",
    )(c.reshape(d, 1), w, b.reshape(1, n))


def _rope_heads(acc, cos, sina, sinb):
    half = ROPE_DIM // 2
    outs = []
    for h in range(MB_HEADS):
        xh = acc[:, h * MB_DH:(h + 1) * MB_DH]
        outs.append(xh * cos + pltpu.roll(xh, MB_DH - half, 1) * sina + pltpu.roll(xh, half, 1) * sinb)
    return outs


def _inproj_kernel(x_ref, mod_ref, w_ref, wg_ref, cos_ref, sina_ref, sinb_ref,
                   q_ref, kaug_ref, v_ref, mqk_ref, mv_ref, mo_ref, g_ref, kmean_ref,
                   u_ref, *, tm, qscale):
    i = pl.program_id(0)
    j = pl.program_id(1)

    @pl.when(j == 0)
    def _():
        u = x_ref[...] * (1.0 + mod_ref[1:2, :]) + mod_ref[0:1, :]
        ub = u.astype(BF16)
        u_ref[...] = ub
        g_ref[...] = jnp.dot(ub, wg_ref[...], preferred_element_type=F32)

    def proj():
        return jnp.dot(u_ref[...], w_ref[...], preferred_element_type=F32)

    @pl.when(j == 0)
    def _():
        rot = _rope_heads(proj(), cos_ref[...], sina_ref[...], sinb_ref[...])
        for h in range(MB_HEADS):
            q_ref[:, h * MB_DH:(h + 1) * MB_DH] = (rot[h] * qscale).astype(BF16)

    @pl.when(j == 1)
    def _():
        rot = _rope_heads(proj(), cos_ref[...], sina_ref[...], sinb_ref[...])
        row = lax.broadcasted_iota(jnp.int32, (tm, V7X_LANES), 0)
        lane = lax.broadcasted_iota(jnp.int32, (tm, V7X_LANES), 1)
        blk = (i * tm + row) // MB_BLOCK
        onehot = jnp.where(lane == blk, 1.0, 0.0).astype(BF16)
        for h in range(MB_HEADS):
            kaug_ref[:, 2 * h * MB_DH:(2 * h + 1) * MB_DH] = rot[h].astype(BF16)
            kaug_ref[:, (2 * h + 1) * MB_DH:(2 * h + 2) * MB_DH] = onehot
            for b in range(tm // MB_BLOCK):
                kmean_ref[0, b:b + 1, h * MB_DH:(h + 1) * MB_DH] = jnp.mean(
                    rot[h][b * MB_BLOCK:(b + 1) * MB_BLOCK, :], axis=0, keepdims=True)

    @pl.when(j == 2)
    def _():
        v_ref[...] = proj().astype(BF16)

    @pl.when(j == 3)
    def _():
        mqk_ref[...] = proj()

    @pl.when(j == 4)
    def _():
        mv_ref[...] = proj().astype(BF16)

    @pl.when(j == 5)
    def _():
        mo_ref[...] = proj()


def _inproj(x, mod, w_main, w_gates, cos, sina, sinb, tm):
    s, d = x.shape
    nbt = tm // MB_BLOCK
    row = lambda i, j: (i, 0)
    out_shape = (
        jax.ShapeDtypeStruct((s, MB_W), BF16),
        jax.ShapeDtypeStruct((s, 2 * MB_W), BF16),
        jax.ShapeDtypeStruct((s, MB_W), BF16),
        jax.ShapeDtypeStruct((s, 2 * ML_QK_W), F32),
        jax.ShapeDtypeStruct((s, ML_V_W), BF16),
        jax.ShapeDtypeStruct((s, ML_V_W), F32),
        jax.ShapeDtypeStruct((s, V7X_LANES), F32),
        jax.ShapeDtypeStruct((s // tm, nbt, MB_W), F32),
    )
    out_specs = (
        pl.BlockSpec((tm, MB_W), row),
        pl.BlockSpec((tm, 2 * MB_W), row),
        pl.BlockSpec((tm, MB_W), row),
        pl.BlockSpec((tm, 2 * ML_QK_W), row),
        pl.BlockSpec((tm, ML_V_W), row),
        pl.BlockSpec((tm, ML_V_W), row),
        pl.BlockSpec((tm, V7X_LANES), row),
        pl.BlockSpec((1, nbt, MB_W), lambda i, j: (i, 0, 0)),
    )
    in_specs = [
        pl.BlockSpec((tm, d), row),
        pl.BlockSpec((8, d), lambda i, j: (0, 0)),
        pl.BlockSpec((d, GROUP_W), lambda i, j: (0, j)),
        pl.BlockSpec((d, V7X_LANES), lambda i, j: (0, 0)),
        pl.BlockSpec((tm, MB_DH), row),
        pl.BlockSpec((tm, MB_DH), row),
        pl.BlockSpec((tm, MB_DH), row),
    ]
    qscale = MB_DH ** -0.5 * LOG2E
    return pl.pallas_call(
        functools.partial(_inproj_kernel, tm=tm, qscale=qscale),
        grid=(s // tm, N_GROUPS),
        in_specs=in_specs,
        out_specs=out_specs,
        out_shape=out_shape,
        scratch_shapes=[pltpu.VMEM((tm, d), BF16)],
        compiler_params=_params(("arbitrary", "arbitrary")),
        name="inproj",
    )(x, mod, w_main, w_gates, cos, sina, sinb)


def _mlstm_kernel(mqk_ref, mv_ref, g_ref, mo_ref, cw_ref, cb_ref, gb_ref, nw_ref, wgu_ref, wdn_ref, wout_ref,
                  o_ref, wgu_o, wdn_o, wout_o, xext_ref, c_ref, n_ref, m_ref, *, lc):
    step = pl.program_id(0)

    wgu_o[...] = wgu_ref[...].astype(BF16)
    wdn_o[...] = wdn_ref[...].astype(BF16)
    wout_o[...] = wout_ref[...].astype(BF16)

    @pl.when(step == 0)
    def _():
        xext_ref[0:8, :] = jnp.zeros((8, 2 * ML_QK_W), F32)
        c_ref[...] = jnp.zeros(c_ref.shape, F32)
        n_ref[...] = jnp.zeros(n_ref.shape, F32)
        m_ref[...] = jnp.zeros(m_ref.shape, F32)

    xext_ref[8:8 + lc, :] = mqk_ref[...]
    conv = cb_ref[...] + cw_ref[CONV_K - 1:CONV_K, :] * mqk_ref[...]
    for j in range(CONV_K - 1):
        off = 8 - (CONV_K - 1) + j
        conv = conv + cw_ref[j:j + 1, :] * xext_ref[off:off + lc, :]
    xext_ref[0:8, :] = xext_ref[lc:lc + 8, :]
    qk = jax.nn.silu(conv)

    gates = g_ref[...] + gb_ref[...]
    lf = jax.nn.log_sigmoid(gates)
    row = lax.broadcasted_iota(jnp.int32, (lc, lc), 0)
    col = lax.broadcasted_iota(jnp.int32, (lc, lc), 1)
    causal = col <= row
    tri = jnp.where(causal, 1.0, 0.0).astype(F32)
    bcum = jnp.dot(tri, lf, precision=lax.Precision.HIGHEST, preferred_element_type=F32)
    lane = lax.broadcasted_iota(jnp.int32, (lc, V7X_LANES), 1)
    packed = jnp.where(lane < ML_HEADS, gates, bcum)
    packed_t = packed.T

    for h in range(ML_HEADS):
        q = qk[:, h * ML_DQK:(h + 1) * ML_DQK]
        k = qk[:, ML_QK_W + h * ML_DQK:ML_QK_W + (h + 1) * ML_DQK] * (ML_DQK ** -0.5)
        v = mv_ref[:, h * ML_DV:(h + 1) * ML_DV]
        qb = q.astype(BF16)
        b_col = bcum[:, ML_HEADS + h:ML_HEADS + h + 1]
        ig_col = gates[:, h:h + 1]
        b_row = packed_t[ML_HEADS + h:ML_HEADS + h + 1, :]
        ig_row = packed_t[h:h + 1, :]
        m_prev = m_ref[h:h + 1, 0:1]
        c_prev = c_ref[h]
        n_prev = n_ref[h:h + 1, :]

        dmat = jnp.where(causal, b_col - (b_row - ig_row), -jnp.inf)
        a = b_col + m_prev
        m_t = jnp.maximum(a, jnp.max(dmat, axis=1, keepdims=True))
        e = jnp.exp(dmat - m_t)
        qkt = lax.dot_general(qb, k.astype(BF16), (((1,), (1,)), ((), ())), preferred_element_type=F32)
        smat = qkt * e
        w_inter = jnp.exp(a - m_t)
        num = (w_inter * jnp.dot(qb, c_prev.astype(BF16), preferred_element_type=F32)
               + jnp.dot(smat.astype(BF16), v, preferred_element_type=F32))
        den = w_inter * jnp.sum(q * n_prev, axis=1, keepdims=True) + jnp.sum(smat, axis=1, keepdims=True)
        hh = num / jnp.maximum(jnp.abs(den), jnp.exp(-m_t))

        mu = jnp.mean(hh, axis=1, keepdims=True)
        hc = hh - mu
        var = jnp.mean(hc * hc, axis=1, keepdims=True)
        sl = slice(h * ML_DV, (h + 1) * ML_DV)
        hn = hc * lax.rsqrt(var + LN_EPS) * nw_ref[:, sl]
        o_ref[:, sl] = (hn * jax.nn.sigmoid(mo_ref[:, sl])).astype(BF16)

        b_last = b_col[lc - 1:lc, :]
        g_col = b_last - b_col + ig_col
        m_new = jnp.maximum(b_last + m_prev, jnp.max(g_col, axis=0, keepdims=True))
        w_k = jnp.exp(g_col - m_new)
        decay = jnp.exp(b_last + m_prev - m_new)
        kw = k * w_k
        upd = lax.dot_general(kw.astype(BF16), v, (((0,), (0,)), ((), ())), preferred_element_type=F32)
        c_ref[h] = decay * c_prev + upd
        n_ref[h:h + 1, :] = decay * n_prev + jnp.sum(kw, axis=0, keepdims=True)
        m_ref[h:h + 1, :] = jnp.broadcast_to(m_new, (1, V7X_LANES))


BF16_SUBLANES = 2 * V7X_SUBLANES


def _cast_slab_rows(rows, n_steps):
    for r in range(BF16_SUBLANES, rows + 1, BF16_SUBLANES):
        if rows % r == 0 and rows // r <= n_steps:
            return r
    raise ValueError((rows, n_steps))


def _mlstm(mqk, mv, gates, mo, conv_w, conv_b, gate_b, norm_w, cast_weights, lc):
    s = mqk.shape[0]
    n_steps = s // lc
    row = lambda i: (i, 0)
    fixed = lambda i: (0, 0)
    cast_specs = []
    for w in cast_weights:
        r = _cast_slab_rows(w.shape[0], n_steps)
        last = w.shape[0] // r - 1
        cast_specs.append(pl.BlockSpec((r, w.shape[1]), functools.partial(
            lambda i, last: (jnp.minimum(i, last), 0), last=last)))
    return pl.pallas_call(
        functools.partial(_mlstm_kernel, lc=lc),
        grid=(n_steps,),
        in_specs=[pl.BlockSpec((lc, 2 * ML_QK_W), row),
                  pl.BlockSpec((lc, ML_V_W), row),
                  pl.BlockSpec((lc, V7X_LANES), row),
                  pl.BlockSpec((lc, ML_V_W), row),
                  pl.BlockSpec((CONV_K, 2 * ML_QK_W), fixed),
                  pl.BlockSpec((1, 2 * ML_QK_W), fixed),
                  pl.BlockSpec((1, V7X_LANES), fixed),
                  pl.BlockSpec((1, ML_V_W), fixed)] + cast_specs,
        out_specs=[pl.BlockSpec((lc, ML_V_W), row)] + cast_specs,
        out_shape=[jax.ShapeDtypeStruct((s, ML_V_W), BF16)]
                  + [jax.ShapeDtypeStruct(w.shape, BF16) for w in cast_weights],
        scratch_shapes=[pltpu.VMEM((lc + 8, 2 * ML_QK_W), F32),
                        pltpu.VMEM((ML_HEADS, ML_DQK, ML_DV), F32),
                        pltpu.VMEM((8, ML_DQK), F32),
                        pltpu.VMEM((8, V7X_LANES), F32)],
        compiler_params=_params(("arbitrary",)),
        name="mlstm",
    )(mqk, mv, gates, mo, conv_w, conv_b, gate_b, norm_w, *cast_weights)


def _moba_kernel(q_ref, kaug_ref, v_ref, km_ref, o_ref, vt_ref, qa_ref, acc_ref, *, tq, tk):
    i = pl.program_id(1)
    nsub = tq // MB_BLOCK

    @pl.when(i == 0)
    def _():
        for c in range(v_ref.shape[0] // tk):
            vt_ref[c] = v_ref[c * tk:(c + 1) * tk, :].astype(F32).T.astype(BF16)

    qt = q_ref[...].astype(F32).T.astype(BF16)
    km = km_ref[0]
    km_hi = km.astype(BF16)
    km_lo = (km - km_hi.astype(F32)).astype(BF16)
    gate = (jnp.dot(km_hi, qt, preferred_element_type=F32)
            + jnp.dot(km_lo, qt, preferred_element_type=F32))
    blk = lax.broadcasted_iota(jnp.int32, gate.shape, 0)
    own = i * nsub + lax.broadcasted_iota(jnp.int32, gate.shape, 1) // MB_BLOCK
    blk_f = blk.astype(F32)
    gm = jnp.where(blk < own, gate, -jnp.inf)
    keep = blk == own
    for r in range(MB_TOPK):
        mx = jnp.max(gm, axis=0, keepdims=True)
        first = jnp.min(jnp.where(gm == mx, blk_f, float(V7X_LANES)), axis=0, keepdims=True)
        pick = blk_f == first
        keep = jnp.logical_or(keep, jnp.logical_and(pick, r < own))
        gm = jnp.where(pick, -jnp.inf, gm)
    qa_ref[:MB_DH, :] = qt
    qa_ref[MB_DH:, :] = jnp.where(keep, 0.0, -MASK_BIG).astype(BF16)
    acc_ref[...] = jnp.zeros(acc_ref.shape, F32)

    def softmax_update(j, st, carry, c0):
        m_old, l_old = carry
        m_new = jnp.maximum(m_old, jnp.max(st, axis=0, keepdims=True))
        alpha = jnp.exp2(m_old - m_new)
        pt = jnp.exp2(st - m_new)
        l_new = alpha * l_old + jnp.sum(pt, axis=0, keepdims=True)
        acc_ref[:, c0:] = alpha * acc_ref[:, c0:] + jnp.dot(vt_ref[j], pt.astype(BF16),
                                                           preferred_element_type=F32)
        return m_new, l_new

    def logits(j, c0):
        start = pl.multiple_of(j * tk, tk)
        return jnp.dot(kaug_ref[pl.ds(start, tk), :], qa_ref[:, c0:], preferred_element_type=F32)

    def past_step(j, carry):
        return softmax_update(j, logits(j, 0), carry, 0)

    kpt = tq // tk
    n_past = i * kpt
    m_run, l_run = lax.fori_loop(0, n_past, past_step,
                                 (jnp.full((1, tq), M_INIT, F32), jnp.zeros((1, tq), F32)))
    for d in range(kpt):
        c0 = d * tk
        st = logits(n_past + d, c0)
        key = lax.broadcasted_iota(jnp.int32, st.shape, 0)
        qry = lax.broadcasted_iota(jnp.int32, st.shape, 1)
        st = jnp.where(key <= qry, st, -MASK_BIG)
        m_sub, l_sub = softmax_update(n_past + d, st, (m_run[:, c0:], l_run[:, c0:]), c0)
        if c0:
            m_run = jnp.concatenate([m_run[:, :c0], m_sub], axis=1)
            l_run = jnp.concatenate([l_run[:, :c0], l_sub], axis=1)
        else:
            m_run, l_run = m_sub, l_sub
    o_ref[...] = (acc_ref[...] / l_run).T.astype(BF16)


def _moba(q, kaug, v, kmean, tq, tk):
    s = q.shape[0]
    assert tq % tk == 0 and tk % MB_BLOCK == 0
    return pl.pallas_call(
        functools.partial(_moba_kernel, tq=tq, tk=tk),
        grid=(MB_HEADS, s // tq),
        in_specs=[pl.BlockSpec((tq, MB_DH), lambda h, i: (i, h)),
                  pl.BlockSpec((s, 2 * MB_DH), lambda h, i: (0, h)),
                  pl.BlockSpec((s, MB_DH), lambda h, i: (0, h)),
                  pl.BlockSpec((1, V7X_LANES, MB_DH), lambda h, i: (h, 0, 0))],
        out_specs=pl.BlockSpec((tq, MB_DH), lambda h, i: (i, h)),
        out_shape=jax.ShapeDtypeStruct((s, MB_W), BF16),
        scratch_shapes=[pltpu.VMEM((s // tk, MB_DH, tk), BF16),
                        pltpu.VMEM((2 * MB_DH, tq), BF16),
                        pltpu.VMEM((MB_DH, tq), F32)],
        compiler_params=_params(("arbitrary", "arbitrary")),
        name="moba",
    )(q, kaug, v, kmean)


def _outproj_kernel(m_ref, a_ref, x_ref, mod_ref, wm_ref, wa_ref, lnw_ref, lnb_ref, o_ref, *, alpha):
    mix = (jnp.dot(m_ref[...], wm_ref[...], preferred_element_type=F32)
           + jnp.dot(a_ref[...], wa_ref[...], preferred_element_type=F32))
    y = alpha * x_ref[...] + (1.0 + mod_ref[2:3, :]) * mix
    o_ref[...] = _layer_norm(y, lnw_ref[...], lnb_ref[...])


def _outproj(m_out, a_out, x, mod, w, ln_w, ln_b, alpha, tm):
    s, d = x.shape
    assert ML_V_W == MB_W and w.shape == (ML_V_W + MB_W, d)
    row = lambda i: (i, 0)
    fixed = lambda i: (0, 0)
    return pl.pallas_call(
        functools.partial(_outproj_kernel, alpha=alpha),
        grid=(s // tm,),
        in_specs=[pl.BlockSpec((tm, ML_V_W), row),
                  pl.BlockSpec((tm, MB_W), row),
                  pl.BlockSpec((tm, d), row),
                  pl.BlockSpec((8, d), fixed),
                  pl.BlockSpec((ML_V_W, d), fixed),
                  pl.BlockSpec((MB_W, d), lambda i: (1, 0)),
                  pl.BlockSpec((1, d), fixed),
                  pl.BlockSpec((1, d), fixed)],
        out_specs=pl.BlockSpec((tm, d), row),
        out_shape=jax.ShapeDtypeStruct((s, d), F32),
        compiler_params=_params(("arbitrary",)),
        name="outproj",
    )(m_out, a_out, x, mod, w, w, ln_w, ln_b)


def _ffn_kernel(x_ref, mod_ref, wg_ref, wu_ref, wd_ref, lnw_ref, lnb_ref, o_ref, u_ref, acc_ref, *, alpha):
    t = pl.program_id(1)
    last = pl.num_programs(1) - 1

    def partial_down(u):
        g = jnp.dot(u, wg_ref[...], preferred_element_type=F32)
        up = jnp.dot(u, wu_ref[...], preferred_element_type=F32)
        hid = (jax.nn.silu(g) * up).astype(BF16)
        return jnp.dot(hid, wd_ref[...], preferred_element_type=F32)

    @pl.when(t == 0)
    def _():
        u = (x_ref[...] * (1.0 + mod_ref[4:5, :]) + mod_ref[3:4, :]).astype(BF16)
        u_ref[...] = u
        acc_ref[...] = partial_down(u)

    @pl.when(jnp.logical_and(t > 0, t < last))
    def _():
        acc_ref[...] += partial_down(u_ref[...])

    @pl.when(t == last)
    def _():
        f = acc_ref[...] + partial_down(u_ref[...])
        y = alpha * x_ref[...] + (1.0 + mod_ref[5:6, :]) * f
        o_ref[...] = _layer_norm(y, lnw_ref[...], lnb_ref[...])


def _ffn(x1, mod, w_gu, w_down, ln_w, ln_b, alpha, tm, tf):
    s, d = x1.shape
    dff = w_down.shape[0]
    nt = dff // tf
    assert nt >= 2
    row = lambda i, t: (i, 0)
    fixed = lambda i, t: (0, 0)
    return pl.pallas_call(
        functools.partial(_ffn_kernel, alpha=alpha),
        grid=(s // tm, nt),
        in_specs=[pl.BlockSpec((tm, d), row),
                  pl.BlockSpec((8, d), fixed),
                  pl.BlockSpec((d, tf), lambda i, t: (0, t)),
                  pl.BlockSpec((d, tf), lambda i, t: (0, t + nt)),
                  pl.BlockSpec((tf, d), lambda i, t: (t, 0)),
                  pl.BlockSpec((1, d), fixed),
                  pl.BlockSpec((1, d), fixed)],
        out_specs=pl.BlockSpec((tm, d), row),
        out_shape=jax.ShapeDtypeStruct((s, d), F32),
        scratch_shapes=[pltpu.VMEM((tm, d), BF16), pltpu.VMEM((tm, d), F32)],
        compiler_params=_params(("arbitrary", "arbitrary")),
        name="ffn",
    )(x1, mod, w_gu, w_gu, w_down, ln_w, ln_b)


def _rope_tables(s):
    half = ROPE_DIM // 2
    inv = ROPE_THETA ** (-np.arange(half, dtype=np.float64) * 2.0 / ROPE_DIM)
    ang = np.arange(s, dtype=np.float64)[:, None] * inv[None, :]
    cos, sin = jnp.asarray(np.cos(ang), F32), jnp.asarray(np.sin(ang), F32)
    zeros = jnp.zeros((s, half), F32)
    pad = MB_DH - ROPE_DIM
    cos_t = jnp.concatenate([cos, cos, jnp.ones((s, pad), F32)], axis=1)
    sina_t = jnp.concatenate([-sin, zeros, jnp.zeros((s, pad), F32)], axis=1)
    sinb_t = jnp.concatenate([zeros, sin, jnp.zeros((s, pad), F32)], axis=1)
    return cos_t, sina_t, sinb_t


def _pick_tile(n, target):
    t = min(n, target)
    assert n % t == 0, (n, t)
    return t


def kernel(x, c, w_ada, b_ada, w_in, conv_w, conv_b, ml_igate_b, ml_fgate_b, ml_norm_w,
           w_out, ln1_w, ln1_b, w_gu, w_down, ln2_w, ln2_b):
    bsz, s, d = x.shape
    depth = w_ada.shape[0]
    dff = w_down.shape[1]
    assert bsz == 1 and c.shape == (1, d)
    assert s % MB_BLOCK == 0 and s // MB_BLOCK <= V7X_LANES
    main_w = N_GROUPS * GROUP_W
    assert w_in.shape[2] == main_w + 2 * ML_HEADS
    alpha = float((2 * depth) ** 0.25)
    tm_in = _pick_tile(s, 512)
    tm_out = _pick_tile(s, 512)
    tm_ffn = _pick_tile(s, 512)
    tf = _pick_tile(dff, 512)
    lc = _pick_tile(s, 256)
    tq_moba = _pick_tile(s, 2048)
    tk_moba = _pick_tile(tq_moba, 512)
    cos_t, sina_t, sinb_t = _rope_tables(s)
    nb = s // MB_BLOCK

    xs = x.reshape(s, d)
    for l in range(depth):
        ada = _ada(c, w_ada[l], b_ada[l])
        mod = jnp.pad(ada.reshape(6, d), ((0, 2), (0, 0)))
        w_main = w_in[l].astype(BF16)
        w_gates = jnp.pad(w_in[l][:, main_w:], ((0, 0), (0, V7X_LANES - 2 * ML_HEADS))).astype(BF16)
        q, kaug, v, mqk, mv, mo, gates, kmean = _inproj(xs, mod, w_main, w_gates, cos_t, sina_t, sinb_t, tm_in)

        gate_b = jnp.pad(jnp.concatenate([ml_igate_b[l], ml_fgate_b[l]]), (0, V7X_LANES - 2 * ML_HEADS))
        m_out, w_gu_b, w_down_b, wo = _mlstm(mqk, mv, gates, mo, conv_w[l], conv_b[l].reshape(1, -1),
                                             gate_b.reshape(1, V7X_LANES), ml_norm_w[l].reshape(1, -1),
                                             (w_gu[l], w_down[l], w_out[l]), lc)

        kmean_h = kmean.reshape(nb, MB_HEADS, MB_DH).transpose(1, 0, 2)
        kmean_h = jnp.pad(kmean_h, ((0, 0), (0, V7X_LANES - nb), (0, 0)))
        a_out = _moba(q, kaug, v, kmean_h, tq_moba, tk_moba)

        x1 = _outproj(m_out, a_out, xs, mod, wo, ln1_w[l].reshape(1, d),
                      ln1_b[l].reshape(1, d), alpha, tm_out)
        xs = _ffn(x1, mod, w_gu_b, w_down_b, ln2_w[l].reshape(1, d),
                  ln2_b[l].reshape(1, d), alpha, tm_ffn, tf)
    return xs.reshape(bsz, s, d)
```
